```python
import jax, jax.numpy as jnp
from jax import lax
import numpy as np

D_MODEL = 1024
BATCH = 16
SEQ = 2048
DEPTH = 2

N_EVEN = (DEPTH + 1) // 2
N_ODD = DEPTH // 2
MIX_HALF = D_MODEL // 2

HGRN_HEADS = 4
HGRN_DK = 128
HGRN_DV = MIX_HALF // HGRN_HEADS
HGRN_CHUNK = 64
HGRN_K = HGRN_HEADS * HGRN_DK
HGRN_V = HGRN_HEADS * HGRN_DV
SGU_GROUPS = 4
SGU_CH = MIX_HALF // SGU_GROUPS
SGU_CHUNK = 128
CONV_CH = MIX_HALF
CONV_WIDTH = 31
MLA_HEADS = 4
MLA_NOPE = 128
MLA_ROPE = 64
MLA_V = 128
MLA_Q_RANK = 384
MLA_KV_RANK = 256
ATTN_BLOCK = 128
ROPE_THETA = 10000.0
D_FF = -(-8 * D_MODEL // (3 * 256)) * 256
EPS = 1e-6

IN_EVEN = 2 * HGRN_K + 2 * HGRN_V + 2 * MIX_HALF
IN_ODD = 2 * CONV_CH + MLA_Q_RANK + MLA_KV_RANK + MLA_ROPE

kernel_name = "hgrn2_gmlp_conformer_mla_hybrid"

F32 = jnp.float32


def rmsnorm(x, w):
    xf = x.astype(F32)
    y = xf * lax.rsqrt(jnp.mean(xf * xf, axis=-1, keepdims=True) + EPS)
    return (y * w.astype(F32)).astype(x.dtype)


def layernorm(x, g, b):
    xf = x.astype(F32)
    mu = jnp.mean(xf, axis=-1, keepdims=True)
    var = jnp.mean(jnp.square(xf - mu), axis=-1, keepdims=True)
    y = (xf - mu) * lax.rsqrt(var + EPS)
    return (y * g.astype(F32) + b.astype(F32)).astype(x.dtype)


def hgrn2(q, f_pre, i, g, lb, gnorm_w):
    B, T, _ = q.shape
    nc = T // HGRN_CHUNK
    lb = lb.astype(F32)
    f = lb + (1.0 - lb) * jax.nn.sigmoid(f_pre.astype(F32))
    log_f = jnp.log(f)
    k = 1.0 - f

    def to_chunks(a, d):
        return a.astype(F32).reshape(B, nc, HGRN_CHUNK, HGRN_HEADS, d).transpose(1, 0, 3, 2, 4)

    qc, kc, lfc = to_chunks(q, HGRN_DK), to_chunks(k, HGRN_DK), to_chunks(log_f, HGRN_DK)
    ic = to_chunks(i, HGRN_DV)
    causal = jnp.tril(jnp.ones((HGRN_CHUNK, HGRN_CHUNK), bool))[:, :, None]

    def step(S, inp):
        qb, kb, lfb, ib = inp
        b = jnp.cumsum(lfb, axis=2)
        o_inter = jnp.einsum('bhtd,bhdv->bhtv', qb * jnp.exp(b), S)
        diff = b[:, :, :, None, :] - b[:, :, None, :, :]
        decay = jnp.exp(jnp.where(causal, diff, -jnp.inf))
        scores = jnp.einsum('bhtd,bhtsd,bhsd->bhts', qb, decay, kb)
        o_intra = jnp.einsum('bhts,bhsv->bhtv', scores, ib)
        b_last = b[:, :, -1:, :]
        S = (jnp.exp(b_last[:, :, 0, :, None]) * S
             + jnp.einsum('bhsd,bhsv->bhdv', kb * jnp.exp(b_last - b), ib))
        return S, o_inter + o_intra

    S0 = jnp.zeros((B, HGRN_HEADS, HGRN_DK, HGRN_DV), F32)
    _, o = lax.scan(step, S0, (qc, kc, lfc, ic))
    o = o.transpose(1, 0, 3, 2, 4).reshape(B, T, HGRN_HEADS, HGRN_DV)
    o = rmsnorm(o, gnorm_w.reshape(HGRN_HEADS, HGRN_DV)).reshape(B, T, HGRN_V)
    return (o * jax.nn.silu(g.astype(F32))).astype(g.dtype)


def spatial_gating(u, v, ln_g, ln_b, w_s, b_s):
    B, T, _ = u.shape
    nc = T // SGU_CHUNK
    vn = layernorm(v.reshape(B, T, SGU_GROUPS, SGU_CH),
                   ln_g.reshape(SGU_GROUPS, SGU_CH), ln_b.reshape(SGU_GROUPS, SGU_CH))
    vn = vn.reshape(B, nc, SGU_CHUNK, SGU_GROUPS, SGU_CH)
    w = w_s * jnp.tril(jnp.ones((SGU_CHUNK, SGU_CHUNK), w_s.dtype))[None]
    z = jnp.einsum('gts,bnsgc->bntgc', w, vn) + b_s.T[:, :, None]
    return u * z.reshape(B, T, SGU_GROUPS * SGU_CH)


def conformer_conv(h_glu, conv_w, conv_b, ln_g, ln_b):
    a, gate = jnp.split(h_glu, 2, axis=-1)
    h = a * jax.nn.sigmoid(gate)
    h = lax.conv_general_dilated(h, conv_w[:, None, :].astype(h.dtype), window_strides=(1,),
                                 padding=[(CONV_WIDTH - 1, 0)],
                                 dimension_numbers=('NWC', 'WIO', 'NWC'),
                                 feature_group_count=CONV_CH) + conv_b
    return jax.nn.silu(layernorm(h, ln_g, ln_b))


def rope_tables(positions):
    inv = 1.0 / (ROPE_THETA ** (jnp.arange(0, MLA_ROPE, 2, dtype=F32) / MLA_ROPE))
    ang = positions.astype(F32)[..., None] * inv
    return jnp.cos(ang), jnp.sin(ang)


def apply_rope(x, cos, sin):
    x1, x2 = jnp.split(x.astype(F32), 2, axis=-1)
    return jnp.concatenate([x1 * cos - x2 * sin, x1 * sin + x2 * cos], axis=-1).astype(x.dtype)


def causal_mla_attention(q_nope, q_rope, k_nope, k_rope, v):
    T = q_nope.shape[1]
    scale = (MLA_NOPE + MLA_ROPE) ** -0.5
    outs = []
    for start in range(0, T, ATTN_BLOCK):
        end = start + ATTN_BLOCK
        s = (jnp.einsum('bqhd,bkhd->bhqk', q_nope[:, start:end], k_nope[:, :end],
                        preferred_element_type=F32)
             + jnp.einsum('bqhr,bkr->bhqk', q_rope[:, start:end], k_rope[:, :end],
                          preferred_element_type=F32)) * scale
        mask = (start + jnp.arange(ATTN_BLOCK))[:, None] >= jnp.arange(end)[None, :]
        p = jax.nn.softmax(jnp.where(mask, s, -jnp.inf), axis=-1)
        outs.append(jnp.einsum('bhqk,bkhv->bqhv', p.astype(v.dtype), v[:, :end]))
    return jnp.concatenate(outs, axis=1)


def mla(c_q, c_kv, k_rope_pre, positions, q_norm, w_uq, kv_norm, w_ukv):
    B, T, _ = c_q.shape
    q = (rmsnorm(c_q, q_norm) @ w_uq).reshape(B, T, MLA_HEADS, MLA_NOPE + MLA_ROPE)
    q_nope, q_rope = q[..., :MLA_NOPE], q[..., MLA_NOPE:]
    kv = (rmsnorm(c_kv, kv_norm) @ w_ukv).reshape(B, T, MLA_HEADS, MLA_NOPE + MLA_V)
    k_nope, v = kv[..., :MLA_NOPE], kv[..., MLA_NOPE:]
    cos, sin = rope_tables(positions)
    q_rope = apply_rope(q_rope, cos[:, :, None, :], sin[:, :, None, :])
    k_rope = apply_rope(k_rope_pre, cos, sin)
    o = causal_mla_attention(q_nope, q_rope, k_nope, k_rope, v)
    return o.reshape(B, T, MLA_HEADS * MLA_V)


def swiglu(h, w_gate, w_up, w_down):
    return (jax.nn.silu(h @ w_gate) * (h @ w_up)) @ w_down


def setup_inputs(seed: int = 0) -> dict:
    key = jax.random.key(seed)
    ks = iter(jax.random.split(key, 32))

    def dense(shape, fan_in):
        return jax.random.normal(next(ks), shape, F32) * fan_in ** -0.5

    def gain(shape):
        return 1.0 + 0.02 * jax.random.normal(next(ks), shape, F32)

    def bias(shape):
        return 0.02 * jax.random.normal(next(ks), shape, F32)

    x = jax.random.normal(next(ks), (BATCH, SEQ, D_MODEL), F32)
    offsets = jax.random.randint(next(ks), (BATCH, 1), 0, 4096, dtype=jnp.int32)
    positions = offsets + jnp.arange(SEQ, dtype=jnp.int32)[None, :]
    return {
        "x": x,
        "positions": positions,
        "mix_norm": gain((DEPTH, D_MODEL)),
        "ffn_norm": gain((DEPTH, D_MODEL)),
        "ffn_gate": dense((DEPTH, D_MODEL, D_FF), D_MODEL),
        "ffn_up": dense((DEPTH, D_MODEL, D_FF), D_MODEL),
        "ffn_down": dense((DEPTH, D_FF, D_MODEL), D_FF),
        "w_in_even": dense((N_EVEN, D_MODEL, IN_EVEN), D_MODEL),
        "w_out_even": dense((N_EVEN, D_MODEL, D_MODEL), D_MODEL),
        "hgrn_lb_logits": jax.random.normal(next(ks), (N_EVEN + 1, HGRN_K), F32),
        "hgrn_gnorm": gain((N_EVEN, HGRN_V)),
        "sgu_ln_g": gain((N_EVEN, MIX_HALF)),
        "sgu_ln_b": bias((N_EVEN, MIX_HALF)),
        "sgu_w": 0.5 * dense((N_EVEN, SGU_GROUPS, SGU_CHUNK, SGU_CHUNK), SGU_CHUNK),
        "sgu_b": gain((N_EVEN, SGU_GROUPS, SGU_CHUNK)),
        "w_in_odd": dense((N_ODD, D_MODEL, IN_ODD), D_MODEL),
        "w_out_odd": dense((N_ODD, D_MODEL, D_MODEL), D_MODEL),
        "conv_w": dense((N_ODD, CONV_WIDTH, CONV_CH), CONV_WIDTH),
        "conv_b": bias((N_ODD, CONV_CH)),
        "conv_ln_g": gain((N_ODD, CONV_CH)),
        "conv_ln_b": bias((N_ODD, CONV_CH)),
        "mla_q_norm": gain((N_ODD, MLA_Q_RANK)),
        "mla_w_uq": dense((N_ODD, MLA_Q_RANK, MLA_HEADS * (MLA_NOPE + MLA_ROPE)), MLA_Q_RANK),
        "mla_kv_norm": gain((N_ODD, MLA_KV_RANK)),
        "mla_w_ukv": dense((N_ODD, MLA_KV_RANK, MLA_HEADS * (MLA_NOPE + MLA_V)), MLA_KV_RANK),
        "final_norm": gain((D_MODEL,)),
    }


def reference(x, positions, mix_norm, ffn_norm, ffn_gate, ffn_up, ffn_down,
              w_in_even, w_out_even, hgrn_lb_logits, hgrn_gnorm, sgu_ln_g, sgu_ln_b, sgu_w, sgu_b,
              w_in_odd, w_out_odd, conv_w, conv_b, conv_ln_g, conv_ln_b,
              mla_q_norm, mla_w_uq, mla_kv_norm, mla_w_ukv, final_norm):
    lower_bounds = jnp.cumsum(jax.nn.softmax(hgrn_lb_logits.astype(F32), axis=0), axis=0)
    even_splits = [HGRN_K, 2 * HGRN_K, 2 * HGRN_K + HGRN_V, 2 * HGRN_K + 2 * HGRN_V,
                   2 * HGRN_K + 2 * HGRN_V + MIX_HALF]
    odd_splits = [2 * CONV_CH, 2 * CONV_CH + MLA_Q_RANK, 2 * CONV_CH + MLA_Q_RANK + MLA_KV_RANK]
    for layer in range(DEPTH):
        j = layer // 2
        h = rmsnorm(x, mix_norm[layer])
        if layer % 2 == 0:
            p = h @ w_in_even[j]
            q, f_pre, i, g, u, v = jnp.split(p, even_splits, axis=-1)
            a_out = hgrn2(q, f_pre, i, g, lower_bounds[j], hgrn_gnorm[j])
            b_out = spatial_gating(jax.nn.gelu(u), jax.nn.gelu(v), sgu_ln_g[j], sgu_ln_b[j],
                                   sgu_w[j], sgu_b[j])
            x = x + jnp.concatenate([a_out, b_out], axis=-1) @ w_out_even[j]
        else:
            p = h @ w_in_odd[j]
            h_glu, c_q, c_kv, k_rope_pre = jnp.split(p, odd_splits, axis=-1)
            c_out = conformer_conv(h_glu, conv_w[j], conv_b[j], conv_ln_g[j], conv_ln_b[j])
            d_out = mla(c_q, c_kv, k_rope_pre, positions, mla_q_norm[j], mla_w_uq[j],
                        mla_kv_norm[j], mla_w_ukv[j])
            x = x + jnp.concatenate([c_out, d_out], axis=-1) @ w_out_odd[j]
        h = rmsnorm(x, ffn_norm[layer])
        x = x + swiglu(h, ffn_gate[layer], ffn_up[layer], ffn_down[layer])
    return rmsnorm(x, final_norm)
```

```python
import functools
import math

import jax
import jax.numpy as jnp
from jax import lax
from jax.experimental import pallas as pl
from jax.experimental.pallas import tpu as pltpu

F32 = jnp.float32
BF16 = jnp.bfloat16

D_MODEL = 1024
MIX_HALF = D_MODEL // 2
HGRN_HEADS = 4
HGRN_DK = 128
HGRN_DV = MIX_HALF // HGRN_HEADS
HGRN_K = HGRN_HEADS * HGRN_DK
HGRN_V = HGRN_HEADS * HGRN_DV
SGU_GROUPS = 4
SGU_CH = MIX_HALF // SGU_GROUPS
SGU_CHUNK = 128
CONV_CH = MIX_HALF
CONV_WIDTH = 31
MLA_HEADS = 4
MLA_NOPE = 128
MLA_ROPE = 64
MLA_V = 128
MLA_Q_RANK = 384
MLA_KV_RANK = 256
ROPE_THETA = 10000.0
D_FF = -(-8 * D_MODEL // (3 * 256)) * 256
EPS = 1e-6
IN_EVEN = 2 * HGRN_K + 2 * HGRN_V + 2 * MIX_HALF
IN_ODD = 2 * CONV_CH + MLA_Q_RANK + MLA_KV_RANK + MLA_ROPE

LANES = 128
SUBLANES = 8
VMEM_LIMIT_BYTES = 56 * 1024 * 1024

MIX_TILE = 256
FFN_TILE = 512
FFN_CHUNKS = ((0, 1024), (1024, 2048), (2048, D_FF))
HGRN_CHUNK = 64
HGRN_SUB = 16
HGRN_NSUB = HGRN_CHUNK // HGRN_SUB
HGRN_GAP = 2 * HGRN_SUB
CONV_HALO = 32
KV_BLOCK = 256
LOG2E = 1.4426950408889634
NEG_BIG = -1e30


def _rms(x, w):
    return x * lax.rsqrt(jnp.mean(x * x, axis=-1, keepdims=True) + EPS) * w


def _dot(a, b):
    return jnp.dot(a, b, preferred_element_type=F32)


def _dot_nt(a, b):
    return lax.dot_general(a, b, (((1,), (1,)), ((), ())), preferred_element_type=F32)


def _sigmoid(x):
    return 1.0 / (1.0 + jnp.exp(-x))


def _gelu_tanh(x):
    return 0.5 * x * (1.0 + jnp.tanh(math.sqrt(2.0 / math.pi) * (x + 0.044715 * (x * x * x))))


def _const_spec(shape):
    nd = len(shape)
    return pl.BlockSpec(shape, lambda *_: (0,) * nd, pipeline_mode=pl.Buffered(1))


def _ffn_kernel(x_ref, nw_ref, wg_ref, wu_ref, wd_ref, fw_ref, o_ref, *, final):
    x = x_ref[...]
    h = _rms(x, nw_ref[...]).astype(BF16)
    acc = x
    for c0, c1 in FFN_CHUNKS:
        g = _dot(h, wg_ref[:, c0:c1])
        u = _dot(h, wu_ref[:, c0:c1])
        a = (g * _sigmoid(g) * u).astype(BF16)
        acc = acc + _dot(a, wd_ref[c0:c1, :])
    if final:
        acc = _rms(acc, fw_ref[...])
    o_ref[...] = acc


def _ffn(x2, norm_w, w_gate, w_up, w_down, final_w, *, final):
    n = x2.shape[0]
    tm = min(FFN_TILE, n)
    return pl.pallas_call(
        functools.partial(_ffn_kernel, final=final),
        grid=(n // tm,),
        in_specs=[
            pl.BlockSpec((tm, D_MODEL), lambda i: (i, 0)),
            _const_spec((1, D_MODEL)),
            _const_spec((D_MODEL, D_FF)),
            _const_spec((D_MODEL, D_FF)),
            _const_spec((D_FF, D_MODEL)),
            _const_spec((1, D_MODEL)),
        ],
        out_specs=pl.BlockSpec((tm, D_MODEL), lambda i: (i, 0)),
        out_shape=jax.ShapeDtypeStruct((n, D_MODEL), F32),
        compiler_params=pltpu.CompilerParams(
            dimension_semantics=("arbitrary",), vmem_limit_bytes=VMEM_LIMIT_BYTES),
        name="ffn_final" if final else "ffn",
    )(x2, norm_w, w_gate, w_up, w_down, final_w)


def _even_kernel(x_ref, nw_ref, win_ref, wout_ref, lb_ref, gn_ref, lng_ref, lnb_ref,
                 sw_ref, sb_ref, o_ref,
                 p_scr, st_scr, kpad, bpad, ipad, mix_scr, *, tt):
    t_idx = pl.program_id(1)
    C, SUB, NSUB, GAP = HGRN_CHUNK, HGRN_SUB, HGRN_NSUB, HGRN_GAP

    @pl.when(t_idx == 0)
    def _():
        st_scr[...] = jnp.zeros_like(st_scr)
        kpad[...] = jnp.zeros_like(kpad)
        bpad[...] = jnp.zeros_like(bpad)
        ipad[...] = jnp.zeros_like(ipad)

    x = x_ref[0]
    h = _rms(x, nw_ref[...]).astype(BF16)
    p_scr[...] = _dot(h, win_ref[...])

    lb = lb_ref[...]
    row = lax.broadcasted_iota(jnp.int32, (C, C), 0)
    col = lax.broadcasted_iota(jnp.int32, (C, C), 1)
    tril_c = (row >= col).astype(F32)

    def chunk_body(c, carry):
        r0 = pl.multiple_of(c * C, C)
        rows = pl.ds(r0, C)
        f_pre = p_scr[rows, HGRN_K:2 * HGRN_K]
        f = lb + (1.0 - lb) * _sigmoid(f_pre)
        lf2 = jnp.log2(f)
        b2_all = jnp.dot(tril_c, lf2, preferred_element_type=F32,
                         precision=lax.Precision.HIGHEST)
        k_all = 1.0 - f
        for hd in range(HGRN_HEADS):
            ksl = slice(hd * HGRN_DK, (hd + 1) * HGRN_DK)
            vsl = slice(hd * HGRN_DV, (hd + 1) * HGRN_DV)
            q = p_scr[rows, ksl]
            iv = p_scr[rows, 2 * HGRN_K + hd * HGRN_DV:2 * HGRN_K + (hd + 1) * HGRN_DV]
            g = p_scr[rows, 2 * HGRN_K + HGRN_V + hd * HGRN_DV:2 * HGRN_K + HGRN_V + (hd + 1) * HGRN_DV]
            b2 = b2_all[:, ksl]
            k = k_all[:, ksl]
            b2_last = b2[C - 1:C, :]
            iv_bf = iv.astype(BF16)

            st = st_scr[hd]
            q_inter = (q * jnp.exp2(b2)).astype(BF16)
            o = _dot_nt(q_inter, st.astype(BF16))

            o_parts = [jnp.zeros((SUB, HGRN_DV), F32)]
            for i in range(1, NSUB):
                r_i = b2[i * SUB - 1:i * SUB, :]
                qt = (q[i * SUB:(i + 1) * SUB] * jnp.exp2(b2[i * SUB:(i + 1) * SUB] - r_i)).astype(BF16)
                kd = (k[:i * SUB] * jnp.exp2(r_i - b2[:i * SUB])).astype(BF16)
                s_i = _dot_nt(qt, kd).astype(BF16)
                o_parts.append(_dot(s_i, iv_bf[:i * SUB]))
            o = o + jnp.concatenate(o_parts, axis=0)

            for i in range(NSUB):
                dst = pl.ds(i * GAP + SUB, SUB)
                src = slice(i * SUB, (i + 1) * SUB)
                kpad[hd, dst, :] = k[src]
                bpad[hd, dst, :] = b2[src]
                ipad[hd, dst, :] = iv[src]
            o = o + jnp.sum(q * k, axis=-1, keepdims=True) * iv
            for d in range(1, SUB):
                def sh(ref):
                    return jnp.concatenate(
                        [ref[hd, pl.ds(i * GAP + SUB - d, SUB), :] for i in range(NSUB)], axis=0)
                e = q * sh(kpad) * jnp.exp2(b2 - sh(bpad))
                o = o + jnp.sum(e, axis=-1, keepdims=True) * sh(ipad)

            k_dec = (k * jnp.exp2(b2_last - b2)).astype(BF16)
            st_scr[hd] = st * jnp.exp2(b2_last) + _dot(iv.T.astype(BF16), k_dec)

            on = _rms(o, gn_ref[:, vsl])
            mix_scr[rows, vsl] = (on * (g * _sigmoid(g))).astype(BF16)
        return carry

    lax.fori_loop(0, tt // C, chunk_body, 0)

    prow = lax.broadcasted_iota(jnp.int32, (SGU_CHUNK, SGU_CHUNK), 0)
    pcol = lax.broadcasted_iota(jnp.int32, (SGU_CHUNK, SGU_CHUNK), 1)
    w_causal = [jnp.where(prow >= pcol, sw_ref[gi], 0.0).astype(BF16) for gi in range(SGU_GROUPS)]
    for n in range(tt // SGU_CHUNK):
        rows = slice(n * SGU_CHUNK, (n + 1) * SGU_CHUNK)
        for gi in range(SGU_GROUPS):
            csl = slice(gi * SGU_CH, (gi + 1) * SGU_CH)
            u = _gelu_tanh(p_scr[rows, 2 * HGRN_K + 2 * HGRN_V + gi * SGU_CH:
                                 2 * HGRN_K + 2 * HGRN_V + (gi + 1) * SGU_CH])
            v = _gelu_tanh(p_scr[rows, 2 * HGRN_K + 2 * HGRN_V + MIX_HALF + gi * SGU_CH:
                                 2 * HGRN_K + 2 * HGRN_V + MIX_HALF + (gi + 1) * SGU_CH])
            mu = jnp.mean(v, axis=-1, keepdims=True)
            vc = v - mu
            var = jnp.mean(vc * vc, axis=-1, keepdims=True)
            vn = vc * lax.rsqrt(var + EPS) * lng_ref[:, csl] + lnb_ref[:, csl]
            z = _dot(w_causal[gi], vn.astype(BF16)) + sb_ref[gi]
            mix_scr[rows, HGRN_V + gi * SGU_CH:HGRN_V + (gi + 1) * SGU_CH] = (u * z).astype(BF16)

    o_ref[0] = x + _dot(mix_scr[...], wout_ref[...])


def _even_mixer(x, norm_w, w_in, w_out, lb, gnorm, ln_g, ln_b, sgu_w, sgu_b_bc):
    bsz, seq, _ = x.shape
    tt = min(MIX_TILE, seq)
    return pl.pallas_call(
        functools.partial(_even_kernel, tt=tt),
        grid=(bsz, seq // tt),
        in_specs=[
            pl.BlockSpec((1, tt, D_MODEL), lambda b, t: (b, t, 0)),
            _const_spec((1, D_MODEL)),
            _const_spec((D_MODEL, IN_EVEN)),
            _const_spec((D_MODEL, D_MODEL)),
            _const_spec((1, HGRN_K)),
            _const_spec((1, HGRN_V)),
            _const_spec((1, MIX_HALF)),
            _const_spec((1, MIX_HALF)),
            _const_spec((SGU_GROUPS, SGU_CHUNK, SGU_CHUNK)),
            _const_spec((SGU_GROUPS, SGU_CHUNK, SGU_CH)),
        ],
        out_specs=pl.BlockSpec((1, tt, D_MODEL), lambda b, t: (b, t, 0)),
        out_shape=jax.ShapeDtypeStruct(x.shape, F32),
        scratch_shapes=[
            pltpu.VMEM((tt, IN_EVEN), F32),
            pltpu.VMEM((HGRN_HEADS, HGRN_DV, HGRN_DK), F32),
            pltpu.VMEM((HGRN_HEADS, HGRN_NSUB * HGRN_GAP, HGRN_DK), F32),
            pltpu.VMEM((HGRN_HEADS, HGRN_NSUB * HGRN_GAP, HGRN_DK), F32),
            pltpu.VMEM((HGRN_HEADS, HGRN_NSUB * HGRN_GAP, HGRN_DV), F32),
            pltpu.VMEM((tt, D_MODEL), BF16),
        ],
        compiler_params=pltpu.CompilerParams(
            dimension_semantics=("arbitrary", "arbitrary"), vmem_limit_bytes=VMEM_LIMIT_BYTES),
        name="even_mixer",
    )(x, norm_w, w_in, w_out, lb, gnorm, ln_g, ln_b, sgu_w, sgu_b_bc)


ODD_CQ = 2 * CONV_CH
ODD_CKV = ODD_CQ + MLA_Q_RANK
ODD_ROPE = ODD_CKV + MLA_KV_RANK
IN_ODD_EXT = ODD_ROPE + 2 * MLA_ROPE
Q_SLOT = MLA_NOPE + 2 * MLA_ROPE


def _odd_kernel(x_ref, pos_ref, nw_ref, win_ref, wout_ref, cw_ref, cb_ref, clg_ref, clb_ref,
                qn_ref, wuq_ref, kvn_ref, wukv_ref, inv_ref, sgn_ref, o_ref,
                p_scr, hpad, k_scr, v_scr, q_scr, mix_scr, *, tt):
    t_idx = pl.program_id(1)
    scale = (MLA_NOPE + MLA_ROPE) ** -0.5

    @pl.when(t_idx == 0)
    def _():
        hpad[0:CONV_HALO, :] = jnp.zeros((CONV_HALO, CONV_CH), F32)

    x = x_ref[0]
    h = _rms(x, nw_ref[...]).astype(BF16)
    p_scr[...] = _dot(h, win_ref[...])

    a = p_scr[:, 0:CONV_CH]
    gate = p_scr[:, CONV_CH:2 * CONV_CH]
    hpad[CONV_HALO:CONV_HALO + tt, :] = a * _sigmoid(gate)
    acc = jnp.zeros((tt, CONV_CH), F32) + cb_ref[...]
    for w in range(CONV_WIDTH):
        acc = acc + hpad[pl.ds(CONV_HALO - (CONV_WIDTH - 1) + w, tt), :] * cw_ref[w:w + 1, :]
    hpad[0:CONV_HALO, :] = hpad[tt:tt + CONV_HALO, :]
    mu = jnp.mean(acc, axis=-1, keepdims=True)
    ac = acc - mu
    var = jnp.mean(ac * ac, axis=-1, keepdims=True)
    cn = ac * lax.rsqrt(var + EPS) * clg_ref[...] + clb_ref[...]
    mix_scr[:, 0:CONV_CH] = (cn * _sigmoid(cn)).astype(BF16)

    cq = _rms(p_scr[:, ODD_CQ:ODD_CKV], qn_ref[...]).astype(BF16)
    ckv = _rms(p_scr[:, ODD_CKV:ODD_ROPE], kvn_ref[...]).astype(BF16)
    qf = _dot(cq, wuq_ref[...])
    kvf = _dot(ckv, wukv_ref[...])

    ang = pos_ref[0] * inv_ref[...]
    cos = jnp.cos(ang)
    sin_s = jnp.sin(ang) * sgn_ref[...]
    rot = jnp.concatenate([cos[:, :MLA_ROPE], sin_s[:, MLA_ROPE:]], axis=1)
    kr = p_scr[:, ODD_ROPE:IN_ODD_EXT] * rot
    kr = kr + pltpu.roll(kr, MLA_ROPE, axis=1)
    q_mult = jnp.concatenate([jnp.full((tt, MLA_NOPE), scale, F32), rot * scale], axis=1)

    r0 = pl.multiple_of(t_idx * tt, tt)
    for hd in range(MLA_HEADS):
        q_scr[hd] = (qf[:, hd * Q_SLOT:(hd + 1) * Q_SLOT] * q_mult).astype(BF16)
        kv0 = hd * (MLA_NOPE + MLA_V)
        k_scr[hd, pl.ds(r0, tt), 0:MLA_NOPE] = kvf[:, kv0:kv0 + MLA_NOPE].astype(BF16)
        k_scr[hd, pl.ds(r0, tt), MLA_NOPE:Q_SLOT] = kr.astype(BF16)
        v_scr[hd, pl.ds(r0, tt), :] = kvf[:, kv0 + MLA_NOPE:kv0 + MLA_NOPE + MLA_V].astype(BF16)

    tk = min(KV_BLOCK, tt)
    nkb = (t_idx + 1) * (tt // tk)
    qpos = r0 + lax.broadcasted_iota(jnp.int32, (tt, tk), 0)
    kofs = lax.broadcasted_iota(jnp.int32, (tt, tk), 1)
    for hd in range(MLA_HEADS):
        q = q_scr[hd]

        def kv_body(kb, carry):
            m, l, acc_o = carry
            k0 = pl.multiple_of(kb * tk, tk)
            s = _dot_nt(q, k_scr[hd, pl.ds(k0, tk), :])
            s = jnp.where(qpos >= k0 + kofs, s, NEG_BIG)
            m_new = jnp.maximum(m, jnp.max(s, axis=-1, keepdims=True))
            alpha = jnp.exp(m - m_new)
            pr = jnp.exp(s - m_new)
            l_new = alpha * l + jnp.sum(pr, axis=-1, keepdims=True)
            acc_new = alpha * acc_o + _dot(pr.astype(BF16), v_scr[hd, pl.ds(k0, tk), :])
            return m_new, l_new, acc_new

        m0 = jnp.full((tt, 1), NEG_BIG, F32)
        l0 = jnp.zeros((tt, 1), F32)
        a0 = jnp.zeros((tt, MLA_V), F32)
        m, l, acc_o = lax.fori_loop(0, nkb, kv_body, (m0, l0, a0))
        mix_scr[:, CONV_CH + hd * MLA_V:CONV_CH + (hd + 1) * MLA_V] = (acc_o / l).astype(BF16)

    o_ref[0] = x + _dot(mix_scr[...], wout_ref[...])


def _odd_mixer(x, pos_bc, norm_w, w_in_ext, w_out, conv_w, conv_b, cln_g, cln_b,
               q_norm, w_uq_ext, kv_norm, w_ukv, inv_row, sgn_row):
    bsz, seq, _ = x.shape
    tt = min(MIX_TILE, seq)
    return pl.pallas_call(
        functools.partial(_odd_kernel, tt=tt),
        grid=(bsz, seq // tt),
        in_specs=[
            pl.BlockSpec((1, tt, D_MODEL), lambda b, t: (b, t, 0)),
            pl.BlockSpec((1, tt, 2 * MLA_ROPE), lambda b, t: (b, t, 0)),
            _const_spec((1, D_MODEL)),
            _const_spec((D_MODEL, IN_ODD_EXT)),
            _const_spec((D_MODEL, D_MODEL)),
            _const_spec((CONV_WIDTH, CONV_CH)),
            _const_spec((1, CONV_CH)),
            _const_spec((1, CONV_CH)),
            _const_spec((1, CONV_CH)),
            _const_spec((1, MLA_Q_RANK)),
            _const_spec((MLA_Q_RANK, MLA_HEADS * Q_SLOT)),
            _const_spec((1, MLA_KV_RANK)),
            _const_spec((MLA_KV_RANK, MLA_HEADS * (MLA_NOPE + MLA_V))),
            _const_spec((1, 2 * MLA_ROPE)),
            _const_spec((1, 2 * MLA_ROPE)),
        ],
        out_specs=pl.BlockSpec((1, tt, D_MODEL), lambda b, t: (b, t, 0)),
        out_shape=jax.ShapeDtypeStruct(x.shape, F32),
        scratch_shapes=[
            pltpu.VMEM((tt, IN_ODD_EXT), F32),
            pltpu.VMEM((CONV_HALO + tt, CONV_CH), F32),
            pltpu.VMEM((MLA_HEADS, seq, Q_SLOT), BF16),
            pltpu.VMEM((MLA_HEADS, seq, MLA_V), BF16),
            pltpu.VMEM((MLA_HEADS, tt, Q_SLOT), BF16),
            pltpu.VMEM((tt, D_MODEL), BF16),
        ],
        compiler_params=pltpu.CompilerParams(
            dimension_semantics=("arbitrary", "arbitrary"), vmem_limit_bytes=VMEM_LIMIT_BYTES),
        name="odd_mixer",
    )(x, pos_bc, norm_w, w_in_ext, w_out, conv_w, conv_b, cln_g, cln_b,
      q_norm, w_uq_ext, kv_norm, w_ukv, inv_row, sgn_row)


def _swap_halves(w):
    half = w.shape[-1] // 2
    return jnp.concatenate([w[..., half:], w[..., :half]], axis=-1)


def kernel(x, positions, mix_norm, ffn_norm, ffn_gate, ffn_up, ffn_down, w_in_even, w_out_even,
           hgrn_lb_logits, hgrn_gnorm, sgu_ln_g, sgu_ln_b, sgu_w, sgu_b, w_in_odd, w_out_odd,
           conv_w, conv_b, conv_ln_g, conv_ln_b, mla_q_norm, mla_w_uq, mla_kv_norm, mla_w_ukv,
           final_norm):
    bsz, seq, _ = x.shape
    depth = mix_norm.shape[0]
    lower_bounds = jnp.cumsum(jax.nn.softmax(hgrn_lb_logits.astype(F32), axis=0), axis=0)

    inv = 1.0 / (ROPE_THETA ** (jnp.arange(0, MLA_ROPE, 2, dtype=F32) / MLA_ROPE))
    inv_row = jnp.tile(inv, 4)[None, :]
    sgn_row = jnp.concatenate([jnp.ones((MLA_ROPE,), F32), -jnp.ones((MLA_ROPE // 2,), F32),
                               jnp.ones((MLA_ROPE // 2,), F32)])[None, :]
    pos_bc = jnp.broadcast_to(positions.astype(F32)[:, :, None], (bsz, seq, 2 * MLA_ROPE))

    row = lambda v: v.reshape(1, -1).astype(F32)
    for layer in range(depth):
        j = layer // 2
        if layer % 2 == 0:
            sgu_b_bc = jnp.broadcast_to(sgu_b[j][:, :, None], (SGU_GROUPS, SGU_CHUNK, SGU_CH))
            x = _even_mixer(x, row(mix_norm[layer]), w_in_even[j].astype(BF16),
                            w_out_even[j].astype(BF16), row(lower_bounds[j]), row(hgrn_gnorm[j]),
                            row(sgu_ln_g[j]), row(sgu_ln_b[j]), sgu_w[j], sgu_b_bc)
        else:
            w_in = w_in_odd[j]
            w_in_ext = jnp.concatenate([w_in, _swap_halves(w_in[:, ODD_ROPE:])], axis=1).astype(BF16)
            wq = mla_w_uq[j].reshape(MLA_Q_RANK, MLA_HEADS, MLA_NOPE + MLA_ROPE)
            wq_ext = jnp.concatenate([wq, _swap_halves(wq[:, :, MLA_NOPE:])], axis=2)
            wq_ext = wq_ext.reshape(MLA_Q_RANK, MLA_HEADS * Q_SLOT).astype(BF16)
            x = _odd_mixer(x, pos_bc, row(mix_norm[layer]), w_in_ext, w_out_odd[j].astype(BF16),
                           conv_w[j], row(conv_b[j]), row(conv_ln_g[j]), row(conv_ln_b[j]),
                           row(mla_q_norm[j]), wq_ext, row(mla_kv_norm[j]),
                           mla_w_ukv[j].astype(BF16), inv_row, sgn_row)
        x2 = _ffn(x.reshape(bsz * seq, D_MODEL), row(ffn_norm[layer]), ffn_gate[layer].astype(BF16),
                  ffn_up[layer].astype(BF16), ffn_down[layer].astype(BF16), row(final_norm),
                  final=(layer == depth - 1))
        x = x2.reshape(bsz, seq, D_MODEL)
    return x
```

```python
import functools
import math

import jax
import jax.numpy as jnp
from jax import lax
from jax.experimental import pallas as pl
from jax.experimental.pallas import tpu as pltpu

F32 = jnp.float32
BF16 = jnp.bfloat16

D_MODEL = 1024
MIX_HALF = D_MODEL // 2
HGRN_HEADS = 4
HGRN_DK = 128
HGRN_DV = MIX_HALF // HGRN_HEADS
HGRN_K = HGRN_HEADS * HGRN_DK
HGRN_V = HGRN_HEADS * HGRN_DV
SGU_GROUPS = 4
SGU_CH = MIX_HALF // SGU_GROUPS
SGU_CHUNK = 128
CONV_CH = MIX_HALF
CONV_WIDTH = 31
MLA_HEADS = 4
MLA_NOPE = 128
MLA_ROPE = 64
MLA_V = 128
MLA_Q_RANK = 384
MLA_KV_RANK = 256
ROPE_THETA = 10000.0
D_FF = -(-8 * D_MODEL // (3 * 256)) * 256
EPS = 1e-6
IN_EVEN = 2 * HGRN_K + 2 * HGRN_V + 2 * MIX_HALF
IN_ODD = 2 * CONV_CH + MLA_Q_RANK + MLA_KV_RANK + MLA_ROPE

LANES = 128
SUBLANES = 8
VMEM_LIMIT_BYTES = 56 * 1024 * 1024

MIX_TILE = 256
FFN_TILE = 512
FFN_CHUNKS = ((0, 1024), (1024, 2048), (2048, D_FF))
HGRN_CHUNK = 64
HGRN_SUB = 16
HGRN_NSUB = HGRN_CHUNK // HGRN_SUB
HGRN_GAP = 2 * HGRN_SUB
CONV_HALO = 32
CONV_OFF = CONV_HALO - (CONV_WIDTH - 1)
LOG2E = 1.4426950408889634
NEG_BIG = -1e30


def _rms(x, w):
    return x * lax.rsqrt(jnp.mean(x * x, axis=-1, keepdims=True) + EPS) * w


def _dot(a, b):
    return jnp.dot(a, b, preferred_element_type=F32)


def _dot_nt(a, b):
    return lax.dot_general(a, b, (((1,), (1,)), ((), ())), preferred_element_type=F32)


def _sigmoid(x):
    return 1.0 / (1.0 + jnp.exp(-x))


def _gelu_tanh(x):
    return 0.5 * x * (1.0 + jnp.tanh(math.sqrt(2.0 / math.pi) * (x + 0.044715 * (x * x * x))))


def _const_spec(shape):
    nd = len(shape)
    return pl.BlockSpec(shape, lambda *_: (0,) * nd, pipeline_mode=pl.Buffered(1))


def _ffn_kernel(x_ref, nw_ref, wg_ref, wu_ref, wd_ref, fw_ref, o_ref, *, final):
    x = x_ref[...]
    h = _rms(x, nw_ref[...]).astype(BF16)
    acc = x
    for c0, c1 in FFN_CHUNKS:
        g = _dot(h, wg_ref[:, c0:c1])
        u = _dot(h, wu_ref[:, c0:c1])
        a = (g * _sigmoid(g) * u).astype(BF16)
        acc = acc + _dot(a, wd_ref[c0:c1, :])
    if final:
        acc = _rms(acc, fw_ref[...])
    o_ref[...] = acc


def _ffn(x2, norm_w, w_gate, w_up, w_down, final_w, *, final):
    n = x2.shape[0]
    tm = min(FFN_TILE, n)
    return pl.pallas_call(
        functools.partial(_ffn_kernel, final=final),
        grid=(n // tm,),
        in_specs=[
            pl.BlockSpec((tm, D_MODEL), lambda i: (i, 0)),
            _const_spec((1, D_MODEL)),
            _const_spec((D_MODEL, D_FF)),
            _const_spec((D_MODEL, D_FF)),
            _const_spec((D_FF, D_MODEL)),
            _const_spec((1, D_MODEL)),
        ],
        out_specs=pl.BlockSpec((tm, D_MODEL), lambda i: (i, 0)),
        out_shape=jax.ShapeDtypeStruct((n, D_MODEL), F32),
        compiler_params=pltpu.CompilerParams(
            dimension_semantics=("arbitrary",), vmem_limit_bytes=VMEM_LIMIT_BYTES),
        name="ffn_final" if final else "ffn",
    )(x2, norm_w, w_gate, w_up, w_down, final_w)


def _even_kernel(x_ref, nw_ref, win_ref, wout_ref, lb_ref, gn_ref, lng_ref, lnb_ref,
                 sw_ref, sb_ref, o_ref,
                 p_scr, st_scr, kpad, bpad, ipad, mix_scr, *, tt):
    t_idx = pl.program_id(1)
    C, SUB, NSUB, GAP = HGRN_CHUNK, HGRN_SUB, HGRN_NSUB, HGRN_GAP

    @pl.when(t_idx == 0)
    def _():
        st_scr[...] = jnp.zeros_like(st_scr)
        kpad[...] = jnp.zeros_like(kpad)
        bpad[...] = jnp.zeros_like(bpad)
        ipad[...] = jnp.zeros_like(ipad)

    x = x_ref[0]
    h = _rms(x, nw_ref[...]).astype(BF16)
    p_scr[...] = _dot(h, win_ref[...])

    lb = lb_ref[...]
    row = lax.broadcasted_iota(jnp.int32, (C, C), 0)
    col = lax.broadcasted_iota(jnp.int32, (C, C), 1)
    tril_c = (row >= col).astype(F32)

    def chunk_body(c, carry):
        r0 = pl.multiple_of(c * C, C)
        rows = pl.ds(r0, C)
        f_pre = p_scr[rows, HGRN_K:2 * HGRN_K]
        f = lb + (1.0 - lb) * _sigmoid(f_pre)
        lf2 = jnp.log2(f)
        b2_all = jnp.dot(tril_c, lf2, preferred_element_type=F32,
                         precision=lax.Precision.HIGHEST)
        k_all = 1.0 - f
        for hd in range(HGRN_HEADS):
            ksl = slice(hd * HGRN_DK, (hd + 1) * HGRN_DK)
            vsl = slice(hd * HGRN_DV, (hd + 1) * HGRN_DV)
            q = p_scr[rows, ksl]
            iv = p_scr[rows, 2 * HGRN_K + hd * HGRN_DV:2 * HGRN_K + (hd + 1) * HGRN_DV]
            g = p_scr[rows, 2 * HGRN_K + HGRN_V + hd * HGRN_DV:2 * HGRN_K + HGRN_V + (hd + 1) * HGRN_DV]
            b2 = b2_all[:, ksl]
            k = k_all[:, ksl]
            b2_last = b2[C - 1:C, :]
            iv_bf = iv.astype(BF16)

            st = st_scr[hd]
            q_inter = (q * jnp.exp2(b2)).astype(BF16)
            o = _dot_nt(q_inter, st.astype(BF16))

            o_parts = [jnp.zeros((SUB, HGRN_DV), F32)]
            for i in range(1, NSUB):
                r_i = b2[i * SUB - 1:i * SUB, :]
                qt = (q[i * SUB:(i + 1) * SUB] * jnp.exp2(b2[i * SUB:(i + 1) * SUB] - r_i)).astype(BF16)
                kd = (k[:i * SUB] * jnp.exp2(r_i - b2[:i * SUB])).astype(BF16)
                s_i = _dot_nt(qt, kd).astype(BF16)
                o_parts.append(_dot(s_i, iv_bf[:i * SUB]))
            o = o + jnp.concatenate(o_parts, axis=0)

            for i in range(NSUB):
                dst = pl.ds(i * GAP + SUB, SUB)
                src = slice(i * SUB, (i + 1) * SUB)
                kpad[hd, dst, :] = k[src]
                bpad[hd, dst, :] = b2[src]
                ipad[hd, dst, :] = iv[src]
            o = o + jnp.sum(q * k, axis=-1, keepdims=True) * iv
            for d in range(1, SUB):
                def sh(ref):
                    return jnp.concatenate(
                        [ref[hd, pl.ds(i * GAP + SUB - d, SUB), :] for i in range(NSUB)], axis=0)
                e = q * sh(kpad) * jnp.exp2(b2 - sh(bpad))
                o = o + jnp.sum(e, axis=-1, keepdims=True) * sh(ipad)

            k_dec = (k * jnp.exp2(b2_last - b2)).astype(BF16)
            st_scr[hd] = st * jnp.exp2(b2_last) + _dot(iv.T.astype(BF16), k_dec)

            on = _rms(o, gn_ref[:, vsl])
            mix_scr[rows, vsl] = (on * (g * _sigmoid(g))).astype(BF16)
        return carry

    lax.fori_loop(0, tt // C, chunk_body, 0)

    prow = lax.broadcasted_iota(jnp.int32, (SGU_CHUNK, SGU_CHUNK), 0)
    pcol = lax.broadcasted_iota(jnp.int32, (SGU_CHUNK, SGU_CHUNK), 1)
    w_causal = [jnp.where(prow >= pcol, sw_ref[gi], 0.0).astype(BF16) for gi in range(SGU_GROUPS)]
    for n in range(tt // SGU_CHUNK):
        rows = slice(n * SGU_CHUNK, (n + 1) * SGU_CHUNK)
        for gi in range(SGU_GROUPS):
            csl = slice(gi * SGU_CH, (gi + 1) * SGU_CH)
            u = _gelu_tanh(p_scr[rows, 2 * HGRN_K + 2 * HGRN_V + gi * SGU_CH:
                                 2 * HGRN_K + 2 * HGRN_V + (gi + 1) * SGU_CH])
            v = _gelu_tanh(p_scr[rows, 2 * HGRN_K + 2 * HGRN_V + MIX_HALF + gi * SGU_CH:
                                 2 * HGRN_K + 2 * HGRN_V + MIX_HALF + (gi + 1) * SGU_CH])
            mu = jnp.mean(v, axis=-1, keepdims=True)
            vc = v - mu
            var = jnp.mean(vc * vc, axis=-1, keepdims=True)
            vn = vc * lax.rsqrt(var + EPS) * lng_ref[:, csl] + lnb_ref[:, csl]
            z = _dot(w_causal[gi], vn.astype(BF16)) + sb_ref[gi]
            mix_scr[rows, HGRN_V + gi * SGU_CH:HGRN_V + (gi + 1) * SGU_CH] = (u * z).astype(BF16)

    o_ref[0] = x + _dot(mix_scr[...], wout_ref[...])


def _even_mixer(x, norm_w, w_in, w_out, lb, gnorm, ln_g, ln_b, sgu_w, sgu_b_bc):
    bsz, seq, _ = x.shape
    tt = min(MIX_TILE, seq)
    return pl.pallas_call(
        functools.partial(_even_kernel, tt=tt),
        grid=(bsz, seq // tt),
        in_specs=[
            pl.BlockSpec((1, tt, D_MODEL), lambda b, t: (b, t, 0)),
            _const_spec((1, D_MODEL)),
            _const_spec((D_MODEL, IN_EVEN)),
            _const_spec((D_MODEL, D_MODEL)),
            _const_spec((1, HGRN_K)),
            _const_spec((1, HGRN_V)),
            _const_spec((1, MIX_HALF)),
            _const_spec((1, MIX_HALF)),
            _const_spec((SGU_GROUPS, SGU_CHUNK, SGU_CHUNK)),
            _const_spec((SGU_GROUPS, SGU_CHUNK, SGU_CH)),
        ],
        out_specs=pl.BlockSpec((1, tt, D_MODEL), lambda b, t: (b, t, 0)),
        out_shape=jax.ShapeDtypeStruct(x.shape, F32),
        scratch_shapes=[
            pltpu.VMEM((tt, IN_EVEN), F32),
            pltpu.VMEM((HGRN_HEADS, HGRN_DV, HGRN_DK), F32),
            pltpu.VMEM((HGRN_HEADS, HGRN_NSUB * HGRN_GAP, HGRN_DK), F32),
            pltpu.VMEM((HGRN_HEADS, HGRN_NSUB * HGRN_GAP, HGRN_DK), F32),
            pltpu.VMEM((HGRN_HEADS, HGRN_NSUB * HGRN_GAP, HGRN_DV), F32),
            pltpu.VMEM((tt, D_MODEL), BF16),
        ],
        compiler_params=pltpu.CompilerParams(
            dimension_semantics=("arbitrary", "arbitrary"), vmem_limit_bytes=VMEM_LIMIT_BYTES),
        name="even_mixer",
    )(x, norm_w, w_in, w_out, lb, gnorm, ln_g, ln_b, sgu_w, sgu_b_bc)


ODD_CQ = 2 * CONV_CH
ODD_CKV = ODD_CQ + MLA_Q_RANK
ODD_ROPE = ODD_CKV + MLA_KV_RANK
IN_ODD_EXT = ODD_ROPE + 2 * MLA_ROPE
Q_SLOT = MLA_NOPE + 2 * MLA_ROPE


def _odd_kernel(x_ref, pos_ref, nw_ref, win_ref, wout_ref, cw_ref, cb_ref, clg_ref, clb_ref,
                qn_ref, wuq_ref, kvn_ref, wukv_ref, inv_ref, sgn_ref, o_ref,
                p_scr, hpad, hshift, k_scr, vt_scr, q_scr, m_scr, l_scr, acc_scr, s_scr, mix_scr,
                *, tt):
    t_idx = pl.program_id(1)
    scale = (MLA_NOPE + MLA_ROPE) ** -0.5

    @pl.when(t_idx == 0)
    def _():
        hpad[0:CONV_HALO, :] = jnp.zeros((CONV_HALO, CONV_CH), F32)

    x = x_ref[0]
    h = _rms(x, nw_ref[...]).astype(BF16)
    p_scr[...] = _dot(h, win_ref[...])

    a = p_scr[:, 0:CONV_CH]
    gate = p_scr[:, CONV_CH:2 * CONV_CH]
    hpad[CONV_HALO:CONV_HALO + tt, :] = a * _sigmoid(gate)
    for r in range(1, SUBLANES):
        hshift[r - 1] = hpad[pl.ds(r, tt + CONV_HALO - SUBLANES), :]
    acc = jnp.zeros((tt, CONV_CH), F32) + cb_ref[...]
    for w in range(CONV_WIDTH):
        r = (CONV_OFF + w) % SUBLANES
        a8 = (CONV_OFF + w) - r
        tap = hpad[a8:a8 + tt, :] if r == 0 else hshift[r - 1, a8:a8 + tt, :]
        acc = acc + tap * cw_ref[w:w + 1, :]
    hpad[0:CONV_HALO, :] = hpad[tt:tt + CONV_HALO, :]
    mu = jnp.mean(acc, axis=-1, keepdims=True)
    ac = acc - mu
    var = jnp.mean(ac * ac, axis=-1, keepdims=True)
    cn = ac * lax.rsqrt(var + EPS) * clg_ref[...] + clb_ref[...]
    mix_scr[:, 0:CONV_CH] = (cn * _sigmoid(cn)).astype(BF16)

    cq = _rms(p_scr[:, ODD_CQ:ODD_CKV], qn_ref[...]).astype(BF16)
    ckv = _rms(p_scr[:, ODD_CKV:ODD_ROPE], kvn_ref[...]).astype(BF16)
    qf = _dot(cq, wuq_ref[...])
    kvf = _dot(ckv, wukv_ref[...])

    ang = pos_ref[0] * inv_ref[...]
    cos = jnp.cos(ang)
    sin_s = jnp.sin(ang) * sgn_ref[...]
    rot = jnp.concatenate([cos[:, :MLA_ROPE], sin_s[:, MLA_ROPE:]], axis=1)
    kr = p_scr[:, ODD_ROPE:IN_ODD_EXT] * rot
    kr = kr + pltpu.roll(kr, MLA_ROPE, axis=1)
    q_mult = jnp.concatenate([jnp.full((tt, MLA_NOPE), scale, F32), rot * scale], axis=1)

    r0 = pl.multiple_of(t_idx * tt, tt)
    for hd in range(MLA_HEADS):
        q_scr[hd] = (qf[:, hd * Q_SLOT:(hd + 1) * Q_SLOT] * q_mult).astype(BF16)
        kv0 = hd * (MLA_NOPE + MLA_V)
        k_scr[hd, pl.ds(r0, tt), 0:MLA_NOPE] = kvf[:, kv0:kv0 + MLA_NOPE].astype(BF16)
        k_scr[hd, pl.ds(r0, tt), MLA_NOPE:Q_SLOT] = kr.astype(BF16)
        vt_scr[hd, t_idx] = kvf[:, kv0 + MLA_NOPE:kv0 + MLA_NOPE + MLA_V].T.astype(BF16)
        m_scr[hd] = jnp.full((1, tt), NEG_BIG, F32)
        l_scr[hd] = jnp.zeros((1, tt), F32)
        acc_scr[hd] = jnp.zeros((MLA_V, tt), F32)

    heads = range(MLA_HEADS)

    def scores_into(kb, slot):
        k0 = pl.multiple_of(kb * tt, tt)
        for hd in heads:
            s_scr[slot, hd] = _dot_nt(k_scr[hd, pl.ds(k0, tt), :], q_scr[hd])

    def attend(kb, masked):
        sts = [s_scr[kb % 2, hd] for hd in heads]
        if not masked:
            scores_into(kb + 1, (kb + 1) % 2)
        if masked:
            krow = lax.broadcasted_iota(jnp.int32, (tt, tt), 0)
            qcol = lax.broadcasted_iota(jnp.int32, (tt, tt), 1)
            sts = [jnp.where(krow <= qcol, st, NEG_BIG) for st in sts]
        m_olds = [m_scr[hd] for hd in heads]
        m_news = [jnp.maximum(m_olds[hd], jnp.max(sts[hd], axis=0, keepdims=True)) for hd in heads]
        alphas = [jnp.exp(m_olds[hd] - m_news[hd]) for hd in heads]
        prs = [jnp.exp(sts[hd] - m_news[hd]) for hd in heads]
        for hd in heads:
            l_scr[hd] = alphas[hd] * l_scr[hd] + jnp.sum(prs[hd], axis=0, keepdims=True)
            m_scr[hd] = m_news[hd]
        pvs = [_dot(vt_scr[hd, kb], prs[hd].astype(BF16)) for hd in heads]
        for hd in heads:
            acc_scr[hd] = acc_scr[hd] * alphas[hd] + pvs[hd]

    def kv_body(kb, carry):
        attend(kb, False)
        return carry

    scores_into(0, 0)
    lax.fori_loop(0, t_idx, kv_body, 0)
    attend(t_idx, True)
    for hd in range(MLA_HEADS):
        o_t = acc_scr[hd] * (1.0 / l_scr[hd])
        mix_scr[:, CONV_CH + hd * MLA_V:CONV_CH + (hd + 1) * MLA_V] = o_t.T.astype(BF16)

    o_ref[0] = x + _dot(mix_scr[...], wout_ref[...])


def _odd_mixer(x, pos_bc, norm_w, w_in_ext, w_out, conv_w, conv_b, cln_g, cln_b,
               q_norm, w_uq_ext, kv_norm, w_ukv, inv_row, sgn_row):
    bsz, seq, _ = x.shape
    tt = min(MIX_TILE, seq)
    return pl.pallas_call(
        functools.partial(_odd_kernel, tt=tt),
        grid=(bsz, seq // tt),
        in_specs=[
            pl.BlockSpec((1, tt, D_MODEL), lambda b, t: (b, t, 0)),
            pl.BlockSpec((1, tt, 2 * MLA_ROPE), lambda b, t: (b, t, 0)),
            _const_spec((1, D_MODEL)),
            _const_spec((D_MODEL, IN_ODD_EXT)),
            _const_spec((D_MODEL, D_MODEL)),
            _const_spec((CONV_WIDTH, CONV_CH)),
            _const_spec((1, CONV_CH)),
            _const_spec((1, CONV_CH)),
            _const_spec((1, CONV_CH)),
            _const_spec((1, MLA_Q_RANK)),
            _const_spec((MLA_Q_RANK, MLA_HEADS * Q_SLOT)),
            _const_spec((1, MLA_KV_RANK)),
            _const_spec((MLA_KV_RANK, MLA_HEADS * (MLA_NOPE + MLA_V))),
            _const_spec((1, 2 * MLA_ROPE)),
            _const_spec((1, 2 * MLA_ROPE)),
        ],
        out_specs=pl.BlockSpec((1, tt, D_MODEL), lambda b, t: (b, t, 0)),
        out_shape=jax.ShapeDtypeStruct(x.shape, F32),
        scratch_shapes=[
            pltpu.VMEM((tt, IN_ODD_EXT), F32),
            pltpu.VMEM((CONV_HALO + tt, CONV_CH), F32),
            pltpu.VMEM((SUBLANES - 1, tt + CONV_HALO - SUBLANES, CONV_CH), F32),
            pltpu.VMEM((MLA_HEADS, seq, Q_SLOT), BF16),
            pltpu.VMEM((MLA_HEADS, seq // tt, MLA_V, tt), BF16),
            pltpu.VMEM((MLA_HEADS, tt, Q_SLOT), BF16),
            pltpu.VMEM((MLA_HEADS, 1, tt), F32),
            pltpu.VMEM((MLA_HEADS, 1, tt), F32),
            pltpu.VMEM((MLA_HEADS, MLA_V, tt), F32),
            pltpu.VMEM((2, MLA_HEADS, tt, tt), F32),
            pltpu.VMEM((tt, D_MODEL), BF16),
        ],
        compiler_params=pltpu.CompilerParams(
            dimension_semantics=("arbitrary", "arbitrary"), vmem_limit_bytes=VMEM_LIMIT_BYTES),
        name="odd_mixer",
    )(x, pos_bc, norm_w, w_in_ext, w_out, conv_w, conv_b, cln_g, cln_b,
      q_norm, w_uq_ext, kv_norm, w_ukv, inv_row, sgn_row)


def _swap_halves(w):
    half = w.shape[-1] // 2
    return jnp.concatenate([w[..., half:], w[..., :half]], axis=-1)


def kernel(x, positions, mix_norm, ffn_norm, ffn_gate, ffn_up, ffn_down, w_in_even, w_out_even,
           hgrn_lb_logits, hgrn_gnorm, sgu_ln_g, sgu_ln_b, sgu_w, sgu_b, w_in_odd, w_out_odd,
           conv_w, conv_b, conv_ln_g, conv_ln_b, mla_q_norm, mla_w_uq, mla_kv_norm, mla_w_ukv,
           final_norm):
    bsz, seq, _ = x.shape
    depth = mix_norm.shape[0]
    lower_bounds = jnp.cumsum(jax.nn.softmax(hgrn_lb_logits.astype(F32), axis=0), axis=0)

    inv = 1.0 / (ROPE_THETA ** (jnp.arange(0, MLA_ROPE, 2, dtype=F32) / MLA_ROPE))
    inv_row = jnp.tile(inv, 4)[None, :]
    sgn_row = jnp.concatenate([jnp.ones((MLA_ROPE,), F32), -jnp.ones((MLA_ROPE // 2,), F32),
                               jnp.ones((MLA_ROPE // 2,), F32)])[None, :]
    pos_bc = jnp.broadcast_to(positions.astype(F32)[:, :, None], (bsz, seq, 2 * MLA_ROPE))

    row = lambda v: v.reshape(1, -1).astype(F32)
    for layer in range(depth):
        j = layer // 2
        if layer % 2 == 0:
            sgu_b_bc = jnp.broadcast_to(sgu_b[j][:, :, None], (SGU_GROUPS, SGU_CHUNK, SGU_CH))
            x = _even_mixer(x, row(mix_norm[layer]), w_in_even[j].astype(BF16),
                            w_out_even[j].astype(BF16), row(lower_bounds[j]), row(hgrn_gnorm[j]),
                            row(sgu_ln_g[j]), row(sgu_ln_b[j]), sgu_w[j], sgu_b_bc)
        else:
            w_in = w_in_odd[j]
            w_in_ext = jnp.concatenate([w_in, _swap_halves(w_in[:, ODD_ROPE:])], axis=1).astype(BF16)
            wq = mla_w_uq[j].reshape(MLA_Q_RANK, MLA_HEADS, MLA_NOPE + MLA_ROPE)
            wq_ext = jnp.concatenate([wq, _swap_halves(wq[:, :, MLA_NOPE:])], axis=2)
            wq_ext = wq_ext.reshape(MLA_Q_RANK, MLA_HEADS * Q_SLOT).astype(BF16)
            x = _odd_mixer(x, pos_bc, row(mix_norm[layer]), w_in_ext, w_out_odd[j].astype(BF16),
                           conv_w[j], row(conv_b[j]), row(conv_ln_g[j]), row(conv_ln_b[j]),
                           row(mla_q_norm[j]), wq_ext, row(mla_kv_norm[j]),
                           mla_w_ukv[j].astype(BF16), inv_row, sgn_row)
        x2 = _ffn(x.reshape(bsz * seq, D_MODEL), row(ffn_norm[layer]), ffn_gate[layer].astype(BF16),
                  ffn_up[layer].astype(BF16), ffn_down[layer].astype(BF16), row(final_norm),
                  final=(layer == depth - 1))
        x = x2.reshape(bsz, seq, D_MODEL)
    return x
```

```python
import functools
import math

import jax
import jax.numpy as jnp
from jax import lax
from jax.experimental import pallas as pl
from jax.experimental.pallas import tpu as pltpu

F32 = jnp.float32
BF16 = jnp.bfloat16

D_MODEL = 1024
MIX_HALF = D_MODEL // 2
HGRN_HEADS = 4
HGRN_DK = 128
HGRN_DV = MIX_HALF // HGRN_HEADS
HGRN_K = HGRN_HEADS * HGRN_DK
HGRN_V = HGRN_HEADS * HGRN_DV
SGU_GROUPS = 4
SGU_CH = MIX_HALF // SGU_GROUPS
SGU_CHUNK = 128
CONV_CH = MIX_HALF
CONV_WIDTH = 31
MLA_HEADS = 4
MLA_NOPE = 128
MLA_ROPE = 64
MLA_V = 128
MLA_Q_RANK = 384
MLA_KV_RANK = 256
ROPE_THETA = 10000.0
D_FF = -(-8 * D_MODEL // (3 * 256)) * 256
EPS = 1e-6
IN_EVEN = 2 * HGRN_K + 2 * HGRN_V + 2 * MIX_HALF
IN_ODD = 2 * CONV_CH + MLA_Q_RANK + MLA_KV_RANK + MLA_ROPE

LANES = 128
SUBLANES = 8
VMEM_LIMIT_BYTES = 56 * 1024 * 1024

MIX_TILE = 256
FFN_TILE = 512
FFN_CHUNKS = ((0, 1024), (1024, 2048), (2048, D_FF))
HGRN_CHUNK = 64
HGRN_LEVELS = (32, 16, 8, 4)
HGRN_DIAG = 4
CONV_HALO = 32
CONV_OFF = CONV_HALO - (CONV_WIDTH - 1)
NEG_BIG = -1e30


def _rms(x, w):
    return x * lax.rsqrt(jnp.mean(x * x, axis=-1, keepdims=True) + EPS) * w


def _dot(a, b):
    return jnp.dot(a, b, preferred_element_type=F32)


def _dot_nt(a, b):
    return lax.dot_general(a, b, (((1,), (1,)), ((), ())), preferred_element_type=F32)


def _sigmoid(x):
    return 1.0 / (1.0 + jnp.exp(-x))


def _gelu_tanh(x):
    return 0.5 * x * (1.0 + jnp.tanh(math.sqrt(2.0 / math.pi) * (x + 0.044715 * (x * x * x))))


def _const_spec(shape):
    nd = len(shape)
    return pl.BlockSpec(shape, lambda *_: (0,) * nd, pipeline_mode=pl.Buffered(1))


def _ffn_kernel(x_ref, nw_ref, wg_ref, wu_ref, wd_ref, fw_ref, o_ref, *, final):
    x = x_ref[...]
    h = _rms(x, nw_ref[...]).astype(BF16)
    acc = x
    for c0, c1 in FFN_CHUNKS:
        g = _dot(h, wg_ref[:, c0:c1])
        u = _dot(h, wu_ref[:, c0:c1])
        a = (g * _sigmoid(g) * u).astype(BF16)
        acc = acc + _dot(a, wd_ref[c0:c1, :])
    if final:
        acc = _rms(acc, fw_ref[...])
    o_ref[...] = acc


def _ffn(x2, norm_w, w_gate, w_up, w_down, final_w, *, final):
    n = x2.shape[0]
    tm = min(FFN_TILE, n)
    return pl.pallas_call(
        functools.partial(_ffn_kernel, final=final),
        grid=(n // tm,),
        in_specs=[
            pl.BlockSpec((tm, D_MODEL), lambda i: (i, 0)),
            _const_spec((1, D_MODEL)),
            _const_spec((D_MODEL, D_FF)),
            _const_spec((D_MODEL, D_FF)),
            _const_spec((D_FF, D_MODEL)),
            _const_spec((1, D_MODEL)),
        ],
        out_specs=pl.BlockSpec((tm, D_MODEL), lambda i: (i, 0)),
        out_shape=jax.ShapeDtypeStruct((n, D_MODEL), F32),
        compiler_params=pltpu.CompilerParams(
            dimension_semantics=("arbitrary",), vmem_limit_bytes=VMEM_LIMIT_BYTES),
        name="ffn_final" if final else "ffn",
    )(x2, norm_w, w_gate, w_up, w_down, final_w)


def _even_kernel(x_ref, nw_ref, win_ref, wout_ref, lb_ref, gn_ref, lng_ref, lnb_ref,
                 sw_ref, sb_ref, o_ref,
                 p_scr, st_scr, kpad, bpad, msk_scr, mix_scr, *, tt):
    t_idx = pl.program_id(1)
    C, PAD = HGRN_CHUNK, SUBLANES
    heads = range(HGRN_HEADS)

    @pl.when(t_idx == 0)
    def _():
        st_scr[...] = jnp.zeros_like(st_scr)
        kpad[0:PAD, :] = jnp.zeros((PAD, HGRN_K), F32)
        bpad[0:PAD, :] = jnp.zeros((PAD, HGRN_K), F32)

    x = x_ref[0]
    h = _rms(x, nw_ref[...]).astype(BF16)
    p_scr[...] = _dot(h, win_ref[...])

    lb = lb_ref[...]
    ti = lax.broadcasted_iota(jnp.int32, (C, C), 0)
    si = lax.broadcasted_iota(jnp.int32, (C, C), 1)
    tril_c = jnp.where(ti >= si, 1.0, 0.0).astype(BF16)
    tril3 = jnp.concatenate([tril_c, tril_c, tril_c], axis=1)
    xr = jnp.bitwise_xor(ti, si)
    for li, hs in enumerate(HGRN_LEVELS):
        own = jnp.logical_and(lax.shift_right_logical(xr, hs.bit_length() - 1) == 1, ti > si)
        msk_scr[li] = jnp.where(own, 1.0, 0.0)
    for d in range(HGRN_DIAG):
        own = jnp.logical_and(ti - si == d, xr < HGRN_DIAG)
        msk_scr[len(HGRN_LEVELS) + d] = jnp.where(own, 1.0, 0.0)

    def chunk_body(c, carry):
        r0 = pl.multiple_of(c * C, C)
        rows = pl.ds(r0, C)
        q_all = p_scr[rows, 0:HGRN_K]
        f_pre = p_scr[rows, HGRN_K:2 * HGRN_K]
        iv_all = p_scr[rows, 2 * HGRN_K:2 * HGRN_K + HGRN_V]
        g_all = p_scr[rows, 2 * HGRN_K + HGRN_V:2 * HGRN_K + 2 * HGRN_V]
        f = lb + (1.0 - lb) * _sigmoid(f_pre)
        lf = jnp.log(f)
        lf_hi = lf.astype(BF16)
        lf_r = lf - lf_hi.astype(F32)
        lf_mid = lf_r.astype(BF16)
        lf_lo = (lf_r - lf_mid.astype(F32)).astype(BF16)
        b_all = _dot(tril3, jnp.concatenate([lf_hi, lf_mid, lf_lo], axis=0))
        k_all = 1.0 - f
        kpad[PAD:PAD + C, :] = k_all
        bpad[PAD:PAD + C, :] = b_all
        b_last = bpad[PAD + C - 1:PAD + C, :]
        q_inter = (q_all * jnp.exp(b_all)).astype(BF16)
        k_dec = (k_all * jnp.exp(b_last - b_all)).astype(BF16)
        dec_row = jnp.exp(b_last)
        iv_bf = iv_all.astype(BF16)
        gate_all = g_all * _sigmoid(g_all)

        lev = []
        for hs in HGRN_LEVELS:
            pivots = [blk * 2 * hs + hs - 1 for blk in range(C // (2 * hs))]
            piv = jnp.concatenate(
                [jnp.broadcast_to(bpad[PAD + p:PAD + p + 1, :], (2 * hs, HGRN_K)) for p in pivots],
                axis=0)
            e = jnp.exp(-jnp.abs(b_all - piv))
            lev.append(((q_all * e).astype(BF16), (k_all * e).astype(BF16)))
        diag = [q_all * k_all]
        for d in range(1, HGRN_DIAG):
            k_sh = kpad[pl.ds(PAD - d, C), :]
            b_sh = bpad[pl.ds(PAD - d, C), :]
            diag.append(q_all * k_sh * jnp.exp(b_all - b_sh))

        for hd in heads:
            sl = slice(hd * HGRN_DK, (hd + 1) * HGRN_DK)
            vsl = slice(hd * HGRN_DV, (hd + 1) * HGRN_DV)
            sc = jnp.zeros((C, C), F32)
            for li in range(len(HGRN_LEVELS)):
                sc = sc + _dot_nt(lev[li][0][:, sl], lev[li][1][:, sl]) * msk_scr[li]
            for d in range(HGRN_DIAG):
                sc = sc + (jnp.sum(diag[d][:, sl], axis=-1, keepdims=True)
                           * msk_scr[len(HGRN_LEVELS) + d])
            st = st_scr[hd]
            o = _dot_nt(q_inter[:, sl], st.astype(BF16)) + _dot(sc.astype(BF16), iv_bf[:, vsl])
            st_scr[hd] = st * dec_row[:, sl] + _dot(iv_all[:, vsl].T.astype(BF16), k_dec[:, sl])
            on = _rms(o, gn_ref[:, vsl])
            mix_scr[rows, vsl] = (on * gate_all[:, vsl]).astype(BF16)
        return carry

    lax.fori_loop(0, tt // C, chunk_body, 0, unroll=True)

    prow = lax.broadcasted_iota(jnp.int32, (SGU_CHUNK, SGU_CHUNK), 0)
    pcol = lax.broadcasted_iota(jnp.int32, (SGU_CHUNK, SGU_CHUNK), 1)
    w_causal = [jnp.where(prow >= pcol, sw_ref[gi], 0.0).astype(BF16) for gi in range(SGU_GROUPS)]
    for n in range(tt // SGU_CHUNK):
        rows = slice(n * SGU_CHUNK, (n + 1) * SGU_CHUNK)
        for gi in range(SGU_GROUPS):
            csl = slice(gi * SGU_CH, (gi + 1) * SGU_CH)
            u = _gelu_tanh(p_scr[rows, 2 * HGRN_K + 2 * HGRN_V + gi * SGU_CH:
                                 2 * HGRN_K + 2 * HGRN_V + (gi + 1) * SGU_CH])
            v = _gelu_tanh(p_scr[rows, 2 * HGRN_K + 2 * HGRN_V + MIX_HALF + gi * SGU_CH:
                                 2 * HGRN_K + 2 * HGRN_V + MIX_HALF + (gi + 1) * SGU_CH])
            mu = jnp.mean(v, axis=-1, keepdims=True)
            vc = v - mu
            var = jnp.mean(vc * vc, axis=-1, keepdims=True)
            vn = vc * lax.rsqrt(var + EPS) * lng_ref[:, csl] + lnb_ref[:, csl]
            z = _dot(w_causal[gi], vn.astype(BF16)) + sb_ref[gi]
            mix_scr[rows, HGRN_V + gi * SGU_CH:HGRN_V + (gi + 1) * SGU_CH] = (u * z).astype(BF16)

    o_ref[0] = x + _dot(mix_scr[...], wout_ref[...])


def _even_mixer(x, norm_w, w_in, w_out, lb, gnorm, ln_g, ln_b, sgu_w, sgu_b_bc):
    bsz, seq, _ = x.shape
    tt = min(MIX_TILE, seq)
    return pl.pallas_call(
        functools.partial(_even_kernel, tt=tt),
        grid=(bsz, seq // tt),
        in_specs=[
            pl.BlockSpec((1, tt, D_MODEL), lambda b, t: (b, t, 0)),
            _const_spec((1, D_MODEL)),
            _const_spec((D_MODEL, IN_EVEN)),
            _const_spec((D_MODEL, D_MODEL)),
            _const_spec((1, HGRN_K)),
            _const_spec((1, HGRN_V)),
            _const_spec((1, MIX_HALF)),
            _const_spec((1, MIX_HALF)),
            _const_spec((SGU_GROUPS, SGU_CHUNK, SGU_CHUNK)),
            _const_spec((SGU_GROUPS, SGU_CHUNK, SGU_CH)),
        ],
        out_specs=pl.BlockSpec((1, tt, D_MODEL), lambda b, t: (b, t, 0)),
        out_shape=jax.ShapeDtypeStruct(x.shape, F32),
        scratch_shapes=[
            pltpu.VMEM((tt, IN_EVEN), F32),
            pltpu.VMEM((HGRN_HEADS, HGRN_DV, HGRN_DK), F32),
            pltpu.VMEM((SUBLANES + HGRN_CHUNK, HGRN_K), F32),
            pltpu.VMEM((SUBLANES + HGRN_CHUNK, HGRN_K), F32),
            pltpu.VMEM((len(HGRN_LEVELS) + HGRN_DIAG, HGRN_CHUNK, HGRN_CHUNK), F32),
            pltpu.VMEM((tt, D_MODEL), BF16),
        ],
        compiler_params=pltpu.CompilerParams(
            dimension_semantics=("arbitrary", "arbitrary"), vmem_limit_bytes=VMEM_LIMIT_BYTES),
        name="even_mixer",
    )(x, norm_w, w_in, w_out, lb, gnorm, ln_g, ln_b, sgu_w, sgu_b_bc)


ODD_CQ = 2 * CONV_CH
ODD_CKV = ODD_CQ + MLA_Q_RANK
ODD_ROPE = ODD_CKV + MLA_KV_RANK
IN_ODD_EXT = ODD_ROPE + 2 * MLA_ROPE
Q_SLOT = MLA_NOPE + 2 * MLA_ROPE


def _odd_kernel(x_ref, pos_ref, nw_ref, win_ref, wout_ref, cw_ref, cb_ref, clg_ref, clb_ref,
                qn_ref, wuq_ref, kvn_ref, wukv_ref, inv_ref, sgn_ref, o_ref,
                p_scr, hpad, hshift, k_scr, vt_scr, q_scr, m_scr, l_scr, acc_scr, s_scr, mix_scr,
                *, tt):
    t_idx = pl.program_id(1)
    scale = (MLA_NOPE + MLA_ROPE) ** -0.5

    @pl.when(t_idx == 0)
    def _():
        hpad[0:CONV_HALO, :] = jnp.zeros((CONV_HALO, CONV_CH), F32)

    x = x_ref[0]
    h = _rms(x, nw_ref[...]).astype(BF16)
    p_scr[...] = _dot(h, win_ref[...])

    a = p_scr[:, 0:CONV_CH]
    gate = p_scr[:, CONV_CH:2 * CONV_CH]
    hpad[CONV_HALO:CONV_HALO + tt, :] = a * _sigmoid(gate)
    for r in range(1, SUBLANES):
        hshift[r - 1] = hpad[pl.ds(r, tt + CONV_HALO - SUBLANES), :]
    acc = jnp.zeros((tt, CONV_CH), F32) + cb_ref[...]
    for w in range(CONV_WIDTH):
        r = (CONV_OFF + w) % SUBLANES
        a8 = (CONV_OFF + w) - r
        tap = hpad[a8:a8 + tt, :] if r == 0 else hshift[r - 1, a8:a8 + tt, :]
        acc = acc + tap * cw_ref[w:w + 1, :]
    hpad[0:CONV_HALO, :] = hpad[tt:tt + CONV_HALO, :]
    mu = jnp.mean(acc, axis=-1, keepdims=True)
    ac = acc - mu
    var = jnp.mean(ac * ac, axis=-1, keepdims=True)
    cn = ac * lax.rsqrt(var + EPS) * clg_ref[...] + clb_ref[...]
    mix_scr[:, 0:CONV_CH] = (cn * _sigmoid(cn)).astype(BF16)

    cq = _rms(p_scr[:, ODD_CQ:ODD_CKV], qn_ref[...]).astype(BF16)
    ckv = _rms(p_scr[:, ODD_CKV:ODD_ROPE], kvn_ref[...]).astype(BF16)
    qf = _dot(cq, wuq_ref[...])
    kvf = _dot(ckv, wukv_ref[...])

    ang = pos_ref[0] * inv_ref[...]
    cos = jnp.cos(ang)
    sin_s = jnp.sin(ang) * sgn_ref[...]
    rot = jnp.concatenate([cos[:, :MLA_ROPE], sin_s[:, MLA_ROPE:]], axis=1)
    kr = p_scr[:, ODD_ROPE:IN_ODD_EXT] * rot
    kr = kr + pltpu.roll(kr, MLA_ROPE, axis=1)
    q_mult = jnp.concatenate([jnp.full((tt, MLA_NOPE), scale, F32), rot * scale], axis=1)

    r0 = pl.multiple_of(t_idx * tt, tt)
    for hd in range(MLA_HEADS):
        q_scr[hd] = (qf[:, hd * Q_SLOT:(hd + 1) * Q_SLOT] * q_mult).astype(BF16)
        kv0 = hd * (MLA_NOPE + MLA_V)
        k_scr[hd, pl.ds(r0, tt), 0:MLA_NOPE] = kvf[:, kv0:kv0 + MLA_NOPE].astype(BF16)
        k_scr[hd, pl.ds(r0, tt), MLA_NOPE:Q_SLOT] = kr.astype(BF16)
        vt_scr[hd, t_idx] = kvf[:, kv0 + MLA_NOPE:kv0 + MLA_NOPE + MLA_V].T.astype(BF16)
        m_scr[hd] = jnp.full((1, tt), NEG_BIG, F32)
        l_scr[hd] = jnp.zeros((1, tt), F32)
        acc_scr[hd] = jnp.zeros((MLA_V, tt), F32)

    heads = range(MLA_HEADS)

    def scores_into(kb, slot):
        k0 = pl.multiple_of(kb * tt, tt)
        for hd in heads:
            s_scr[slot, hd] = _dot_nt(k_scr[hd, pl.ds(k0, tt), :], q_scr[hd])

    def attend(kb, masked):
        sts = [s_scr[kb % 2, hd] for hd in heads]
        if not masked:
            scores_into(kb + 1, (kb + 1) % 2)
        if masked:
            krow = lax.broadcasted_iota(jnp.int32, (tt, tt), 0)
            qcol = lax.broadcasted_iota(jnp.int32, (tt, tt), 1)
            sts = [jnp.where(krow <= qcol, st, NEG_BIG) for st in sts]
        m_olds = [m_scr[hd] for hd in heads]
        m_news = [jnp.maximum(m_olds[hd], jnp.max(sts[hd], axis=0, keepdims=True)) for hd in heads]
        alphas = [jnp.exp(m_olds[hd] - m_news[hd]) for hd in heads]
        prs = [jnp.exp(sts[hd] - m_news[hd]) for hd in heads]
        for hd in heads:
            l_scr[hd] = alphas[hd] * l_scr[hd] + jnp.sum(prs[hd], axis=0, keepdims=True)
            m_scr[hd] = m_news[hd]
        pvs = [_dot(vt_scr[hd, kb], prs[hd].astype(BF16)) for hd in heads]
        for hd in heads:
            acc_scr[hd] = acc_scr[hd] * alphas[hd] + pvs[hd]

    def kv_body(kb, carry):
        attend(kb, False)
        return carry

    scores_into(0, 0)
    lax.fori_loop(0, t_idx, kv_body, 0)
    attend(t_idx, True)
    for hd in range(MLA_HEADS):
        o_t = acc_scr[hd] * (1.0 / l_scr[hd])
        mix_scr[:, CONV_CH + hd * MLA_V:CONV_CH + (hd + 1) * MLA_V] = o_t.T.astype(BF16)

    o_ref[0] = x + _dot(mix_scr[...], wout_ref[...])


def _odd_mixer(x, pos_bc, norm_w, w_in_ext, w_out, conv_w, conv_b, cln_g, cln_b,
               q_norm, w_uq_ext, kv_norm, w_ukv, inv_row, sgn_row):
    bsz, seq, _ = x.shape
    tt = min(MIX_TILE, seq)
    return pl.pallas_call(
        functools.partial(_odd_kernel, tt=tt),
        grid=(bsz, seq // tt),
        in_specs=[
            pl.BlockSpec((1, tt, D_MODEL), lambda b, t: (b, t, 0)),
            pl.BlockSpec((1, tt, 2 * MLA_ROPE), lambda b, t: (b, t, 0)),
            _const_spec((1, D_MODEL)),
            _const_spec((D_MODEL, IN_ODD_EXT)),
            _const_spec((D_MODEL, D_MODEL)),
            _const_spec((CONV_WIDTH, CONV_CH)),
            _const_spec((1, CONV_CH)),
            _const_spec((1, CONV_CH)),
            _const_spec((1, CONV_CH)),
            _const_spec((1, MLA_Q_RANK)),
            _const_spec((MLA_Q_RANK, MLA_HEADS * Q_SLOT)),
            _const_spec((1, MLA_KV_RANK)),
            _const_spec((MLA_KV_RANK, MLA_HEADS * (MLA_NOPE + MLA_V))),
            _const_spec((1, 2 * MLA_ROPE)),
            _const_spec((1, 2 * MLA_ROPE)),
        ],
        out_specs=pl.BlockSpec((1, tt, D_MODEL), lambda b, t: (b, t, 0)),
        out_shape=jax.ShapeDtypeStruct(x.shape, F32),
        scratch_shapes=[
            pltpu.VMEM((tt, IN_ODD_EXT), F32),
            pltpu.VMEM((CONV_HALO + tt, CONV_CH), F32),
            pltpu.VMEM((SUBLANES - 1, tt + CONV_HALO - SUBLANES, CONV_CH), F32),
            pltpu.VMEM((MLA_HEADS, seq, Q_SLOT), BF16),
            pltpu.VMEM((MLA_HEADS, seq // tt, MLA_V, tt), BF16),
            pltpu.VMEM((MLA_HEADS, tt, Q_SLOT), BF16),
            pltpu.VMEM((MLA_HEADS, 1, tt), F32),
            pltpu.VMEM((MLA_HEADS, 1, tt), F32),
            pltpu.VMEM((MLA_HEADS, MLA_V, tt), F32),
            pltpu.VMEM((2, MLA_HEADS, tt, tt), F32),
            pltpu.VMEM((tt, D_MODEL), BF16),
        ],
        compiler_params=pltpu.CompilerParams(
            dimension_semantics=("arbitrary", "arbitrary"), vmem_limit_bytes=VMEM_LIMIT_BYTES),
        name="odd_mixer",
    )(x, pos_bc, norm_w, w_in_ext, w_out, conv_w, conv_b, cln_g, cln_b,
      q_norm, w_uq_ext, kv_norm, w_ukv, inv_row, sgn_row)


def _swap_halves(w):
    half = w.shape[-1] // 2
    return jnp.concatenate([w[..., half:], w[..., :half]], axis=-1)


def kernel(x, positions, mix_norm, ffn_norm, ffn_gate, ffn_up, ffn_down, w_in_even, w_out_even,
           hgrn_lb_logits, hgrn_gnorm, sgu_ln_g, sgu_ln_b, sgu_w, sgu_b, w_in_odd, w_out_odd,
           conv_w, conv_b, conv_ln_g, conv_ln_b, mla_q_norm, mla_w_uq, mla_kv_norm, mla_w_ukv,
           final_norm):
    bsz, seq, _ = x.shape
    depth = mix_norm.shape[0]
    lower_bounds = jnp.cumsum(jax.nn.softmax(hgrn_lb_logits.astype(F32), axis=0), axis=0)

    inv = 1.0 / (ROPE_THETA ** (jnp.arange(0, MLA_ROPE, 2, dtype=F32) / MLA_ROPE))
    inv_row = jnp.tile(inv, 4)[None, :]
    sgn_row = jnp.concatenate([jnp.ones((MLA_ROPE,), F32), -jnp.ones((MLA_ROPE // 2,), F32),
                               jnp.ones((MLA_ROPE // 2,), F32)])[None, :]
    pos_bc = jnp.broadcast_to(positions.astype(F32)[:, :, None], (bsz, seq, 2 * MLA_ROPE))

    row = lambda v: v.reshape(1, -1).astype(F32)
    for layer in range(depth):
        j = layer // 2
        if layer % 2 == 0:
            sgu_b_bc = jnp.broadcast_to(sgu_b[j][:, :, None], (SGU_GROUPS, SGU_CHUNK, SGU_CH))
            x = _even_mixer(x, row(mix_norm[layer]), w_in_even[j].astype(BF16),
                            w_out_even[j].astype(BF16), row(lower_bounds[j]), row(hgrn_gnorm[j]),
                            row(sgu_ln_g[j]), row(sgu_ln_b[j]), sgu_w[j], sgu_b_bc)
        else:
            w_in = w_in_odd[j]
            w_in_ext = jnp.concatenate([w_in, _swap_halves(w_in[:, ODD_ROPE:])], axis=1).astype(BF16)
            wq = mla_w_uq[j].reshape(MLA_Q_RANK, MLA_HEADS, MLA_NOPE + MLA_ROPE)
            wq_ext = jnp.concatenate([wq, _swap_halves(wq[:, :, MLA_NOPE:])], axis=2)
            wq_ext = wq_ext.reshape(MLA_Q_RANK, MLA_HEADS * Q_SLOT).astype(BF16)
            x = _odd_mixer(x, pos_bc, row(mix_norm[layer]), w_in_ext, w_out_odd[j].astype(BF16),
                           conv_w[j], row(conv_b[j]), row(conv_ln_g[j]), row(conv_ln_b[j]),
                           row(mla_q_norm[j]), wq_ext, row(mla_kv_norm[j]),
                           mla_w_ukv[j].astype(BF16), inv_row, sgn_row)
        x2 = _ffn(x.reshape(bsz * seq, D_MODEL), row(ffn_norm[layer]), ffn_gate[layer].astype(BF16),
                  ffn_up[layer].astype(BF16), ffn_down[layer].astype(BF16), row(final_norm),
                  final=(layer == depth - 1))
        x = x2.reshape(bsz, seq, D_MODEL)
    return x
```

```python
import functools
import math

import jax
import jax.numpy as jnp
from jax import lax
from jax.experimental import pallas as pl
from jax.experimental.pallas import tpu as pltpu

F32 = jnp.float32
BF16 = jnp.bfloat16

D_MODEL = 1024
MIX_HALF = D_MODEL // 2
HGRN_HEADS = 4
HGRN_DK = 128
HGRN_DV = MIX_HALF // HGRN_HEADS
HGRN_K = HGRN_HEADS * HGRN_DK
HGRN_V = HGRN_HEADS * HGRN_DV
SGU_GROUPS = 4
SGU_CH = MIX_HALF // SGU_GROUPS
SGU_CHUNK = 128
CONV_CH = MIX_HALF
CONV_WIDTH = 31
MLA_HEADS = 4
MLA_NOPE = 128
MLA_ROPE = 64
MLA_V = 128
MLA_Q_RANK = 384
MLA_KV_RANK = 256
ROPE_THETA = 10000.0
D_FF = -(-8 * D_MODEL // (3 * 256)) * 256
EPS = 1e-6
IN_EVEN = 2 * HGRN_K + 2 * HGRN_V + 2 * MIX_HALF
IN_ODD = 2 * CONV_CH + MLA_Q_RANK + MLA_KV_RANK + MLA_ROPE

LANES = 128
SUBLANES = 8
VMEM_LIMIT_BYTES = 56 * 1024 * 1024

MIX_TILE = 256
FFN_TILE = 512
FFN_CHUNKS = ((0, 768), (768, 1536), (1536, 2304), (2304, D_FF))
FFN_UP_BLOCK = 512
FFN_DOWN_BLOCK = 512
HGRN_CHUNK = 64
HGRN_LEVELS = (32, 16, 8, 4)
HGRN_DIAG = 4
CONV_HALO = 32
CONV_OFF = CONV_HALO - (CONV_WIDTH - 1)
NEG_BIG = -1e30


def _rms(x, w):
    return x * lax.rsqrt(jnp.mean(x * x, axis=-1, keepdims=True) + EPS) * w


def _dot(a, b):
    return jnp.dot(a, b, preferred_element_type=F32)


def _dot_nt(a, b):
    return lax.dot_general(a, b, (((1,), (1,)), ((), ())), preferred_element_type=F32)


def _sigmoid(x):
    return 1.0 / (1.0 + jnp.exp(-x))


def _gelu_tanh(x):
    return 0.5 * x * (1.0 + jnp.tanh(math.sqrt(2.0 / math.pi) * (x + 0.044715 * (x * x * x))))


def _const_spec(shape):
    nd = len(shape)
    return pl.BlockSpec(shape, lambda *_: (0,) * nd, pipeline_mode=pl.Buffered(1))


class _TrailingFfn:
    def __init__(self, x_prev, nw_ref, wg_ref, wu_ref, wd_ref, act_scr, out_fn):
        self.x, self.wg_ref, self.wu_ref, self.wd_ref = x_prev, wg_ref, wu_ref, wd_ref
        self.act_scr, self.out_fn = act_scr, out_fn
        self.h = _rms(x_prev, nw_ref[...]).astype(BF16)
        up = [(self._up, c0) for c0 in range(0, D_FF, FFN_UP_BLOCK)]
        down = [(self._down, n0) for n0 in range(0, D_MODEL, FFN_DOWN_BLOCK)]
        self.todo = up + down

    def _up(self, c0):
        cols = slice(c0, min(c0 + FFN_UP_BLOCK, D_FF))
        g = _dot(self.h, self.wg_ref[:, cols])
        u = _dot(self.h, self.wu_ref[:, cols])
        self.act_scr[:, cols] = (g * _sigmoid(g) * u).astype(BF16)

    def _down(self, n0):
        cols = slice(n0, n0 + FFN_DOWN_BLOCK)
        self.out_fn(cols, self.x[:, cols] + _dot(self.act_scr[...], self.wd_ref[:, cols]))

    def step(self, n=1):
        for _ in range(min(n, len(self.todo))):
            fn, arg = self.todo.pop(0)
            fn(arg)

    def finish(self):
        self.step(len(self.todo))


def _ffn_kernel(x_ref, nw_ref, wg_ref, wu_ref, wd_ref, fw_ref, o_ref, *, final):
    x = x_ref[...]
    h = _rms(x, nw_ref[...]).astype(BF16)
    acc = x
    for c0, c1 in FFN_CHUNKS:
        g = _dot(h, wg_ref[:, c0:c1])
        u = _dot(h, wu_ref[:, c0:c1])
        a = (g * _sigmoid(g) * u).astype(BF16)
        acc = acc + _dot(a, wd_ref[c0:c1, :])
    if final:
        acc = _rms(acc, fw_ref[...])
    o_ref[...] = acc


def _ffn(x2, norm_w, w_gate, w_up, w_down, final_w, *, final):
    n = x2.shape[0]
    tm = min(FFN_TILE, n)
    return pl.pallas_call(
        functools.partial(_ffn_kernel, final=final),
        grid=(n // tm,),
        in_specs=[
            pl.BlockSpec((tm, D_MODEL), lambda i: (i, 0)),
            _const_spec((1, D_MODEL)),
            _const_spec((D_MODEL, D_FF)),
            _const_spec((D_MODEL, D_FF)),
            _const_spec((D_FF, D_MODEL)),
            _const_spec((1, D_MODEL)),
        ],
        out_specs=pl.BlockSpec((tm, D_MODEL), lambda i: (i, 0)),
        out_shape=jax.ShapeDtypeStruct((n, D_MODEL), F32),
        compiler_params=pltpu.CompilerParams(
            dimension_semantics=("arbitrary",), vmem_limit_bytes=VMEM_LIMIT_BYTES),
        name="ffn_final" if final else "ffn",
    )(x2, norm_w, w_gate, w_up, w_down, final_w)


def _even_kernel(x_ref, nw_ref, win_ref, wout_ref, lb_ref, gn_ref, lng_ref, lnb_ref,
                 sw_ref, sb_ref, fnw_ref, wg_ref, wu_ref, wd_ref, o_ref,
                 p_scr, st_scr, kpad, bpad, msk_scr, mix_scr, xm_scr, act_scr, *, tt, nt, steps):
    s_idx = pl.program_id(0)
    t_idx = lax.rem(jnp.minimum(s_idx, steps - 1), nt)
    slot = lax.rem(s_idx, 2)
    C, PAD = HGRN_CHUNK, SUBLANES
    heads = range(HGRN_HEADS)

    @pl.when(s_idx == 0)
    def _():
        xm_scr[1] = jnp.zeros((tt, D_MODEL), F32)

    @pl.when(t_idx == 0)
    def _():
        st_scr[...] = jnp.zeros_like(st_scr)
        kpad[0:PAD, :] = jnp.zeros((PAD, HGRN_K), F32)
        bpad[0:PAD, :] = jnp.zeros((PAD, HGRN_K), F32)

    def ffn_out(cols, val):
        o_ref[0, :, cols] = val

    ffn = _TrailingFfn(xm_scr[1 - slot], fnw_ref, wg_ref, wu_ref, wd_ref, act_scr, ffn_out)
    x = x_ref[0]
    h = _rms(x, nw_ref[...]).astype(BF16)
    p_scr[...] = _dot(h, win_ref[...])

    lb = lb_ref[...]
    ti = lax.broadcasted_iota(jnp.int32, (C, C), 0)
    si = lax.broadcasted_iota(jnp.int32, (C, C), 1)
    tril_c = jnp.where(ti >= si, 1.0, 0.0).astype(BF16)
    tril3 = jnp.concatenate([tril_c, tril_c, tril_c], axis=1)
    xr = jnp.bitwise_xor(ti, si)
    for li, hs in enumerate(HGRN_LEVELS):
        own = jnp.logical_and(lax.shift_right_logical(xr, hs.bit_length() - 1) == 1, ti > si)
        msk_scr[li] = jnp.where(own, 1.0, 0.0)
    for d in range(HGRN_DIAG):
        own = jnp.logical_and(ti - si == d, xr < HGRN_DIAG)
        msk_scr[len(HGRN_LEVELS) + d] = jnp.where(own, 1.0, 0.0)

    for c in range(tt // C):
        ffn.step()
        rows = slice(c * C, (c + 1) * C)
        q_all = p_scr[rows, 0:HGRN_K]
        f_pre = p_scr[rows, HGRN_K:2 * HGRN_K]
        iv_all = p_scr[rows, 2 * HGRN_K:2 * HGRN_K + HGRN_V]
        g_all = p_scr[rows, 2 * HGRN_K + HGRN_V:2 * HGRN_K + 2 * HGRN_V]
        f = lb + (1.0 - lb) * _sigmoid(f_pre)
        lf = jnp.log(f)
        lf_hi = lf.astype(BF16)
        lf_r = lf - lf_hi.astype(F32)
        lf_mid = lf_r.astype(BF16)
        lf_lo = (lf_r - lf_mid.astype(F32)).astype(BF16)
        b_all = _dot(tril3, jnp.concatenate([lf_hi, lf_mid, lf_lo], axis=0))
        k_all = 1.0 - f
        kpad[PAD:PAD + C, :] = k_all
        bpad[PAD:PAD + C, :] = b_all
        b_last = bpad[PAD + C - 1:PAD + C, :]
        q_inter = (q_all * jnp.exp(b_all)).astype(BF16)
        k_dec = (k_all * jnp.exp(b_last - b_all)).astype(BF16)
        dec_row = jnp.exp(b_last)
        iv_bf = iv_all.astype(BF16)
        gate_all = g_all * _sigmoid(g_all)

        lev = []
        for hs in HGRN_LEVELS:
            pivots = [blk * 2 * hs + hs - 1 for blk in range(C // (2 * hs))]
            piv = jnp.concatenate(
                [jnp.broadcast_to(bpad[PAD + p:PAD + p + 1, :], (2 * hs, HGRN_K)) for p in pivots],
                axis=0)
            e = jnp.exp(-jnp.abs(b_all - piv))
            lev.append(((q_all * e).astype(BF16), (k_all * e).astype(BF16)))
        diag = [q_all * k_all]
        for d in range(1, HGRN_DIAG):
            k_sh = kpad[pl.ds(PAD - d, C), :]
            b_sh = bpad[pl.ds(PAD - d, C), :]
            diag.append(q_all * k_sh * jnp.exp(b_all - b_sh))

        for hd in heads:
            if hd % 2 == 0:
                ffn.step()
            sl = slice(hd * HGRN_DK, (hd + 1) * HGRN_DK)
            vsl = slice(hd * HGRN_DV, (hd + 1) * HGRN_DV)
            sc = jnp.zeros((C, C), F32)
            for li in range(len(HGRN_LEVELS)):
                sc = sc + _dot_nt(lev[li][0][:, sl], lev[li][1][:, sl]) * msk_scr[li]
            for d in range(HGRN_DIAG):
                sc = sc + (jnp.sum(diag[d][:, sl], axis=-1, keepdims=True)
                           * msk_scr[len(HGRN_LEVELS) + d])
            st = st_scr[hd]
            o = _dot_nt(q_inter[:, sl], st.astype(BF16)) + _dot(sc.astype(BF16), iv_bf[:, vsl])
            st_scr[hd] = st * dec_row[:, sl] + _dot(iv_all[:, vsl].T.astype(BF16), k_dec[:, sl])
            on = _rms(o, gn_ref[:, vsl])
            mix_scr[rows, vsl] = (on * gate_all[:, vsl]).astype(BF16)

    prow = lax.broadcasted_iota(jnp.int32, (SGU_CHUNK, SGU_CHUNK), 0)
    pcol = lax.broadcasted_iota(jnp.int32, (SGU_CHUNK, SGU_CHUNK), 1)
    w_causal = [jnp.where(prow >= pcol, sw_ref[gi], 0.0).astype(BF16) for gi in range(SGU_GROUPS)]
    for n in range(tt // SGU_CHUNK):
        rows = slice(n * SGU_CHUNK, (n + 1) * SGU_CHUNK)
        for gi in range(SGU_GROUPS):
            if gi % 2 == 0:
                ffn.step()
            csl = slice(gi * SGU_CH, (gi + 1) * SGU_CH)
            u = _gelu_tanh(p_scr[rows, 2 * HGRN_K + 2 * HGRN_V + gi * SGU_CH:
                                 2 * HGRN_K + 2 * HGRN_V + (gi + 1) * SGU_CH])
            v = _gelu_tanh(p_scr[rows, 2 * HGRN_K + 2 * HGRN_V + MIX_HALF + gi * SGU_CH:
                                 2 * HGRN_K + 2 * HGRN_V + MIX_HALF + (gi + 1) * SGU_CH])
            mu = jnp.mean(v, axis=-1, keepdims=True)
            vc = v - mu
            var = jnp.mean(vc * vc, axis=-1, keepdims=True)
            vn = vc * lax.rsqrt(var + EPS) * lng_ref[:, csl] + lnb_ref[:, csl]
            z = _dot(w_causal[gi], vn.astype(BF16)) + sb_ref[gi]
            mix_scr[rows, HGRN_V + gi * SGU_CH:HGRN_V + (gi + 1) * SGU_CH] = (u * z).astype(BF16)

    xm_scr[slot] = x + _dot(mix_scr[...], wout_ref[...])
    ffn.finish()


def _tile_index_maps(nt, steps):
    def mixer_tile(s):
        m = jnp.minimum(s, steps - 1)
        return m // nt, m % nt, 0

    def ffn_tile(s):
        f = jnp.maximum(s - 1, 0)
        return f // nt, f % nt, 0

    return mixer_tile, ffn_tile


def _even_layer(x, norm_w, w_in, w_out, lb, gnorm, ln_g, ln_b, sgu_w, sgu_b_bc,
                ffn_norm_w, w_gate, w_up, w_down):
    bsz, seq, _ = x.shape
    tt = min(MIX_TILE, seq)
    nt = seq // tt
    steps = bsz * nt
    mixer_tile, ffn_tile = _tile_index_maps(nt, steps)
    return pl.pallas_call(
        functools.partial(_even_kernel, tt=tt, nt=nt, steps=steps),
        grid=(steps + 1,),
        in_specs=[
            pl.BlockSpec((1, tt, D_MODEL), mixer_tile),
            _const_spec((1, D_MODEL)),
            _const_spec((D_MODEL, IN_EVEN)),
            _const_spec((D_MODEL, D_MODEL)),
            _const_spec((1, HGRN_K)),
            _const_spec((1, HGRN_V)),
            _const_spec((1, MIX_HALF)),
            _const_spec((1, MIX_HALF)),
            _const_spec((SGU_GROUPS, SGU_CHUNK, SGU_CHUNK)),
            _const_spec((SGU_GROUPS, SGU_CHUNK, SGU_CH)),
            _const_spec((1, D_MODEL)),
            _const_spec((D_MODEL, D_FF)),
            _const_spec((D_MODEL, D_FF)),
            _const_spec((D_FF, D_MODEL)),
        ],
        out_specs=pl.BlockSpec((1, tt, D_MODEL), ffn_tile),
        out_shape=jax.ShapeDtypeStruct(x.shape, F32),
        scratch_shapes=[
            pltpu.VMEM((tt, IN_EVEN), F32),
            pltpu.VMEM((HGRN_HEADS, HGRN_DV, HGRN_DK), F32),
            pltpu.VMEM((SUBLANES + HGRN_CHUNK, HGRN_K), F32),
            pltpu.VMEM((SUBLANES + HGRN_CHUNK, HGRN_K), F32),
            pltpu.VMEM((len(HGRN_LEVELS) + HGRN_DIAG, HGRN_CHUNK, HGRN_CHUNK), F32),
            pltpu.VMEM((tt, D_MODEL), BF16),
            pltpu.VMEM((2, tt, D_MODEL), F32),
            pltpu.VMEM((tt, D_FF), BF16),
        ],
        compiler_params=pltpu.CompilerParams(
            dimension_semantics=("arbitrary",), vmem_limit_bytes=VMEM_LIMIT_BYTES),
        name="even_layer",
    )(x, norm_w, w_in, w_out, lb, gnorm, ln_g, ln_b, sgu_w, sgu_b_bc,
      ffn_norm_w, w_gate, w_up, w_down)


ODD_CQ = 2 * CONV_CH
ODD_CKV = ODD_CQ + MLA_Q_RANK
ODD_ROPE = ODD_CKV + MLA_KV_RANK
IN_ODD_EXT = ODD_ROPE + 2 * MLA_ROPE
Q_SLOT = MLA_NOPE + 2 * MLA_ROPE


def _odd_kernel(x_ref, pos_ref, nw_ref, win_ref, wout_ref, cw_ref, cb_ref, clg_ref, clb_ref,
                qn_ref, wuq_ref, kvn_ref, wukv_ref, inv_ref, sgn_ref, o_ref,
                p_scr, hpad, hshift, k_scr, vt_scr, q_scr, m_scr, l_scr, acc_scr, s_scr, mix_scr,
                *, tt):
    t_idx = pl.program_id(1)
    scale = (MLA_NOPE + MLA_ROPE) ** -0.5

    @pl.when(t_idx == 0)
    def _():
        hpad[0:CONV_HALO, :] = jnp.zeros((CONV_HALO, CONV_CH), F32)

    x = x_ref[0]
    h = _rms(x, nw_ref[...]).astype(BF16)
    p_scr[...] = _dot(h, win_ref[...])

    a = p_scr[:, 0:CONV_CH]
    gate = p_scr[:, CONV_CH:2 * CONV_CH]
    hpad[CONV_HALO:CONV_HALO + tt, :] = a * _sigmoid(gate)
    for r in range(1, SUBLANES):
        hshift[r - 1] = hpad[pl.ds(r, tt + CONV_HALO - SUBLANES), :]
    acc = jnp.zeros((tt, CONV_CH), F32) + cb_ref[...]
    for w in range(CONV_WIDTH):
        r = (CONV_OFF + w) % SUBLANES
        a8 = (CONV_OFF + w) - r
        tap = hpad[a8:a8 + tt, :] if r == 0 else hshift[r - 1, a8:a8 + tt, :]
        acc = acc + tap * cw_ref[w:w + 1, :]
    hpad[0:CONV_HALO, :] = hpad[tt:tt + CONV_HALO, :]
    mu = jnp.mean(acc, axis=-1, keepdims=True)
    ac = acc - mu
    var = jnp.mean(ac * ac, axis=-1, keepdims=True)
    cn = ac * lax.rsqrt(var + EPS) * clg_ref[...] + clb_ref[...]
    mix_scr[:, 0:CONV_CH] = (cn * _sigmoid(cn)).astype(BF16)

    cq = _rms(p_scr[:, ODD_CQ:ODD_CKV], qn_ref[...]).astype(BF16)
    ckv = _rms(p_scr[:, ODD_CKV:ODD_ROPE], kvn_ref[...]).astype(BF16)
    qf = _dot(cq, wuq_ref[...])
    kvf = _dot(ckv, wukv_ref[...])

    ang = pos_ref[0] * inv_ref[...]
    cos = jnp.cos(ang)
    sin_s = jnp.sin(ang) * sgn_ref[...]
    rot = jnp.concatenate([cos[:, :MLA_ROPE], sin_s[:, MLA_ROPE:]], axis=1)
    kr = p_scr[:, ODD_ROPE:IN_ODD_EXT] * rot
    kr = kr + pltpu.roll(kr, MLA_ROPE, axis=1)
    q_mult = jnp.concatenate([jnp.full((tt, MLA_NOPE), scale, F32), rot * scale], axis=1)

    r0 = pl.multiple_of(t_idx * tt, tt)
    for hd in range(MLA_HEADS):
        q_scr[hd] = (qf[:, hd * Q_SLOT:(hd + 1) * Q_SLOT] * q_mult).astype(BF16)
        kv0 = hd * (MLA_NOPE + MLA_V)
        k_scr[hd, pl.ds(r0, tt), 0:MLA_NOPE] = kvf[:, kv0:kv0 + MLA_NOPE].astype(BF16)
        k_scr[hd, pl.ds(r0, tt), MLA_NOPE:Q_SLOT] = kr.astype(BF16)
        vt_scr[hd, t_idx] = kvf[:, kv0 + MLA_NOPE:kv0 + MLA_NOPE + MLA_V].T.astype(BF16)
        m_scr[hd] = jnp.full((1, tt), NEG_BIG, F32)
        l_scr[hd] = jnp.zeros((1, tt), F32)
        acc_scr[hd] = jnp.zeros((MLA_V, tt), F32)

    heads = range(MLA_HEADS)

    def scores_into(kb, slot):
        k0 = pl.multiple_of(kb * tt, tt)
        for hd in heads:
            s_scr[slot, hd] = _dot_nt(k_scr[hd, pl.ds(k0, tt), :], q_scr[hd])

    def attend(kb, masked):
        sts = [s_scr[kb % 2, hd] for hd in heads]
        if not masked:
            scores_into(kb + 1, (kb + 1) % 2)
        if masked:
            krow = lax.broadcasted_iota(jnp.int32, (tt, tt), 0)
            qcol = lax.broadcasted_iota(jnp.int32, (tt, tt), 1)
            sts = [jnp.where(krow <= qcol, st, NEG_BIG) for st in sts]
        m_olds = [m_scr[hd] for hd in heads]
        m_news = [jnp.maximum(m_olds[hd], jnp.max(sts[hd], axis=0, keepdims=True)) for hd in heads]
        alphas = [jnp.exp(m_olds[hd] - m_news[hd]) for hd in heads]
        prs = [jnp.exp(sts[hd] - m_news[hd]) for hd in heads]
        for hd in heads:
            l_scr[hd] = alphas[hd] * l_scr[hd] + jnp.sum(prs[hd], axis=0, keepdims=True)
            m_scr[hd] = m_news[hd]
        pvs = [_dot(vt_scr[hd, kb], prs[hd].astype(BF16)) for hd in heads]
        for hd in heads:
            acc_scr[hd] = acc_scr[hd] * alphas[hd] + pvs[hd]

    def kv_body(kb, carry):
        attend(kb, False)
        return carry

    scores_into(0, 0)
    lax.fori_loop(0, t_idx, kv_body, 0)
    attend(t_idx, True)
    for hd in range(MLA_HEADS):
        o_t = acc_scr[hd] * (1.0 / l_scr[hd])
        mix_scr[:, CONV_CH + hd * MLA_V:CONV_CH + (hd + 1) * MLA_V] = o_t.T.astype(BF16)

    o_ref[0] = x + _dot(mix_scr[...], wout_ref[...])


def _odd_mixer(x, pos_bc, norm_w, w_in_ext, w_out, conv_w, conv_b, cln_g, cln_b,
               q_norm, w_uq_ext, kv_norm, w_ukv, inv_row, sgn_row):
    bsz, seq, _ = x.shape
    tt = min(MIX_TILE, seq)
    return pl.pallas_call(
        functools.partial(_odd_kernel, tt=tt),
        grid=(bsz, seq // tt),
        in_specs=[
            pl.BlockSpec((1, tt, D_MODEL), lambda b, t: (b, t, 0)),
            pl.BlockSpec((1, tt, 2 * MLA_ROPE), lambda b, t: (b, t, 0)),
            _const_spec((1, D_MODEL)),
            _const_spec((D_MODEL, IN_ODD_EXT)),
            _const_spec((D_MODEL, D_MODEL)),
            _const_spec((CONV_WIDTH, CONV_CH)),
            _const_spec((1, CONV_CH)),
            _const_spec((1, CONV_CH)),
            _const_spec((1, CONV_CH)),
            _const_spec((1, MLA_Q_RANK)),
            _const_spec((MLA_Q_RANK, MLA_HEADS * Q_SLOT)),
            _const_spec((1, MLA_KV_RANK)),
            _const_spec((MLA_KV_RANK, MLA_HEADS * (MLA_NOPE + MLA_V))),
            _const_spec((1, 2 * MLA_ROPE)),
            _const_spec((1, 2 * MLA_ROPE)),
        ],
        out_specs=pl.BlockSpec((1, tt, D_MODEL), lambda b, t: (b, t, 0)),
        out_shape=jax.ShapeDtypeStruct(x.shape, F32),
        scratch_shapes=[
            pltpu.VMEM((tt, IN_ODD_EXT), F32),
            pltpu.VMEM((CONV_HALO + tt, CONV_CH), F32),
            pltpu.VMEM((SUBLANES - 1, tt + CONV_HALO - SUBLANES, CONV_CH), F32),
            pltpu.VMEM((MLA_HEADS, seq, Q_SLOT), BF16),
            pltpu.VMEM((MLA_HEADS, seq // tt, MLA_V, tt), BF16),
            pltpu.VMEM((MLA_HEADS, tt, Q_SLOT), BF16),
            pltpu.VMEM((MLA_HEADS, 1, tt), F32),
            pltpu.VMEM((MLA_HEADS, 1, tt), F32),
            pltpu.VMEM((MLA_HEADS, MLA_V, tt), F32),
            pltpu.VMEM((2, MLA_HEADS, tt, tt), F32),
            pltpu.VMEM((tt, D_MODEL), BF16),
        ],
        compiler_params=pltpu.CompilerParams(
            dimension_semantics=("arbitrary", "arbitrary"), vmem_limit_bytes=VMEM_LIMIT_BYTES),
        name="odd_mixer",
    )(x, pos_bc, norm_w, w_in_ext, w_out, conv_w, conv_b, cln_g, cln_b,
      q_norm, w_uq_ext, kv_norm, w_ukv, inv_row, sgn_row)


def _swap_halves(w):
    half = w.shape[-1] // 2
    return jnp.concatenate([w[..., half:], w[..., :half]], axis=-1)


def kernel(x, positions, mix_norm, ffn_norm, ffn_gate, ffn_up, ffn_down, w_in_even, w_out_even,
           hgrn_lb_logits, hgrn_gnorm, sgu_ln_g, sgu_ln_b, sgu_w, sgu_b, w_in_odd, w_out_odd,
           conv_w, conv_b, conv_ln_g, conv_ln_b, mla_q_norm, mla_w_uq, mla_kv_norm, mla_w_ukv,
           final_norm):
    bsz, seq, _ = x.shape
    depth = mix_norm.shape[0]
    lower_bounds = jnp.cumsum(jax.nn.softmax(hgrn_lb_logits.astype(F32), axis=0), axis=0)

    inv = 1.0 / (ROPE_THETA ** (jnp.arange(0, MLA_ROPE, 2, dtype=F32) / MLA_ROPE))
    inv_row = jnp.tile(inv, 4)[None, :]
    sgn_row = jnp.concatenate([jnp.ones((MLA_ROPE,), F32), -jnp.ones((MLA_ROPE // 2,), F32),
                               jnp.ones((MLA_ROPE // 2,), F32)])[None, :]
    pos_bc = jnp.broadcast_to(positions.astype(F32)[:, :, None], (bsz, seq, 2 * MLA_ROPE))

    row = lambda v: v.reshape(1, -1).astype(F32)
    for layer in range(depth):
        j = layer // 2
        if layer % 2 == 0:
            sgu_b_bc = jnp.broadcast_to(sgu_b[j][:, :, None], (SGU_GROUPS, SGU_CHUNK, SGU_CH))
            x = _even_layer(x, row(mix_norm[layer]), w_in_even[j].astype(BF16),
                            w_out_even[j].astype(BF16), row(lower_bounds[j]), row(hgrn_gnorm[j]),
                            row(sgu_ln_g[j]), row(sgu_ln_b[j]), sgu_w[j], sgu_b_bc,
                            row(ffn_norm[layer]), ffn_gate[layer].astype(BF16),
                            ffn_up[layer].astype(BF16), ffn_down[layer].astype(BF16))
            continue
        else:
            w_in = w_in_odd[j]
            w_in_ext = jnp.concatenate([w_in, _swap_halves(w_in[:, ODD_ROPE:])], axis=1).astype(BF16)
            wq = mla_w_uq[j].reshape(MLA_Q_RANK, MLA_HEADS, MLA_NOPE + MLA_ROPE)
            wq_ext = jnp.concatenate([wq, _swap_halves(wq[:, :, MLA_NOPE:])], axis=2)
            wq_ext = wq_ext.reshape(MLA_Q_RANK, MLA_HEADS * Q_SLOT).astype(BF16)
            x = _odd_mixer(x, pos_bc, row(mix_norm[layer]), w_in_ext, w_out_odd[j].astype(BF16),
                           conv_w[j], row(conv_b[j]), row(conv_ln_g[j]), row(conv_ln_b[j]),
                           row(mla_q_norm[j]), wq_ext, row(mla_kv_norm[j]),
                           mla_w_ukv[j].astype(BF16), inv_row, sgn_row)
        x2 = _ffn(x.reshape(bsz * seq, D_MODEL), row(ffn_norm[layer]), ffn_gate[layer].astype(BF16),
                  ffn_up[layer].astype(BF16), ffn_down[layer].astype(BF16), row(final_norm),
                  final=(layer == depth - 1))
        x = x2.reshape(bsz, seq, D_MODEL)
    return x
```

```python
import functools
import math

import jax
import jax.numpy as jnp
from jax import lax
from jax.experimental import pallas as pl
from jax.experimental.pallas import tpu as pltpu

F32 = jnp.float32
BF16 = jnp.bfloat16

D_MODEL = 1024
MIX_HALF = D_MODEL // 2
HGRN_HEADS = 4
HGRN_DK = 128
HGRN_DV = MIX_HALF // HGRN_HEADS
HGRN_K = HGRN_HEADS * HGRN_DK
HGRN_V = HGRN_HEADS * HGRN_DV
SGU_GROUPS = 4
SGU_CH = MIX_HALF // SGU_GROUPS
SGU_CHUNK = 128
CONV_CH = MIX_HALF
CONV_WIDTH = 31
MLA_HEADS = 4
MLA_NOPE = 128
MLA_ROPE = 64
MLA_V = 128
MLA_Q_RANK = 384
MLA_KV_RANK = 256
ROPE_THETA = 10000.0
D_FF = -(-8 * D_MODEL // (3 * 256)) * 256
EPS = 1e-6
IN_EVEN = 2 * HGRN_K + 2 * HGRN_V + 2 * MIX_HALF
IN_ODD = 2 * CONV_CH + MLA_Q_RANK + MLA_KV_RANK + MLA_ROPE

LANES = 128
SUBLANES = 8
VMEM_LIMIT_BYTES = 56 * 1024 * 1024

MIX_TILE = 256
ODD_TILE = 512
FFN_TILE = 512
FFN_CHUNKS = ((0, 768), (768, 1536), (1536, 2304), (2304, D_FF))
FFN_UP_BLOCK = 512
FFN_DOWN_BLOCK = 512
HGRN_CHUNK = 64
HGRN_LEVELS = (32, 16, 8, 4)
HGRN_DIAG = 4
CONV_HALO = 32
CONV_OFF = CONV_HALO - (CONV_WIDTH - 1)
NEG_BIG = -1e30


def _rms(x, w):
    return x * lax.rsqrt(jnp.mean(x * x, axis=-1, keepdims=True) + EPS) * w


def _dot(a, b):
    return jnp.dot(a, b, preferred_element_type=F32)


def _dot_nt(a, b):
    return lax.dot_general(a, b, (((1,), (1,)), ((), ())), preferred_element_type=F32)


def _sigmoid(x):
    return 1.0 / (1.0 + jnp.exp(-x))


def _gelu_tanh(x):
    return 0.5 * x * (1.0 + jnp.tanh(math.sqrt(2.0 / math.pi) * (x + 0.044715 * (x * x * x))))


def _const_spec(shape):
    nd = len(shape)
    return pl.BlockSpec(shape, lambda *_: (0,) * nd, pipeline_mode=pl.Buffered(1))


class _TrailingFfn:
    def __init__(self, x_prev, nw_ref, wg_ref, wu_ref, wd_ref, act_scr, out_fn):
        self.x, self.wg_ref, self.wu_ref, self.wd_ref = x_prev, wg_ref, wu_ref, wd_ref
        self.act_scr, self.out_fn = act_scr, out_fn
        self.h = _rms(x_prev, nw_ref[...]).astype(BF16)
        up = [(self._up, c0) for c0 in range(0, D_FF, FFN_UP_BLOCK)]
        down = [(self._down, n0) for n0 in range(0, D_MODEL, FFN_DOWN_BLOCK)]
        self.todo = up + down

    def _up(self, c0):
        cols = slice(c0, min(c0 + FFN_UP_BLOCK, D_FF))
        g = _dot(self.h, self.wg_ref[:, cols])
        u = _dot(self.h, self.wu_ref[:, cols])
        self.act_scr[:, cols] = (g * _sigmoid(g) * u).astype(BF16)

    def _down(self, n0):
        cols = slice(n0, n0 + FFN_DOWN_BLOCK)
        self.out_fn(cols, self.x[:, cols] + _dot(self.act_scr[...], self.wd_ref[:, cols]))

    def step(self, n=1):
        for _ in range(min(n, len(self.todo))):
            fn, arg = self.todo.pop(0)
            fn(arg)

    def finish(self):
        self.step(len(self.todo))


def _ffn_kernel(x_ref, nw_ref, wg_ref, wu_ref, wd_ref, fw_ref, o_ref, *, final):
    x = x_ref[...]
    h = _rms(x, nw_ref[...]).astype(BF16)
    acc = x
    for c0, c1 in FFN_CHUNKS:
        g = _dot(h, wg_ref[:, c0:c1])
        u = _dot(h, wu_ref[:, c0:c1])
        a = (g * _sigmoid(g) * u).astype(BF16)
        acc = acc + _dot(a, wd_ref[c0:c1, :])
    if final:
        acc = _rms(acc, fw_ref[...])
    o_ref[...] = acc


def _ffn(x2, norm_w, w_gate, w_up, w_down, final_w, *, final):
    n = x2.shape[0]
    tm = min(FFN_TILE, n)
    return pl.pallas_call(
        functools.partial(_ffn_kernel, final=final),
        grid=(n // tm,),
        in_specs=[
            pl.BlockSpec((tm, D_MODEL), lambda i: (i, 0)),
            _const_spec((1, D_MODEL)),
            _const_spec((D_MODEL, D_FF)),
            _const_spec((D_MODEL, D_FF)),
            _const_spec((D_FF, D_MODEL)),
            _const_spec((1, D_MODEL)),
        ],
        out_specs=pl.BlockSpec((tm, D_MODEL), lambda i: (i, 0)),
        out_shape=jax.ShapeDtypeStruct((n, D_MODEL), F32),
        compiler_params=pltpu.CompilerParams(
            dimension_semantics=("arbitrary",), vmem_limit_bytes=VMEM_LIMIT_BYTES),
        name="ffn_final" if final else "ffn",
    )(x2, norm_w, w_gate, w_up, w_down, final_w)


def _even_kernel(x_ref, nw_ref, win_ref, wout_ref, lb_ref, gn_ref, lng_ref, lnb_ref,
                 sw_ref, sb_ref, fnw_ref, wg_ref, wu_ref, wd_ref, o_ref,
                 p_scr, st_scr, kpad, bpad, msk_scr, mix_scr, xm_scr, act_scr, *, tt, nt, steps):
    s_idx = pl.program_id(0)
    t_idx = lax.rem(jnp.minimum(s_idx, steps - 1), nt)
    slot = lax.rem(s_idx, 2)
    C, PAD = HGRN_CHUNK, SUBLANES
    heads = range(HGRN_HEADS)

    @pl.when(s_idx == 0)
    def _():
        xm_scr[1] = jnp.zeros((tt, D_MODEL), F32)

    @pl.when(t_idx == 0)
    def _():
        st_scr[...] = jnp.zeros_like(st_scr)
        kpad[0:PAD, :] = jnp.zeros((PAD, HGRN_K), F32)
        bpad[0:PAD, :] = jnp.zeros((PAD, HGRN_K), F32)

    def ffn_out(cols, val):
        o_ref[0, :, cols] = val

    ffn = _TrailingFfn(xm_scr[1 - slot], fnw_ref, wg_ref, wu_ref, wd_ref, act_scr, ffn_out)
    x = x_ref[0]
    h = _rms(x, nw_ref[...]).astype(BF16)
    p_scr[...] = _dot(h, win_ref[...])

    lb = lb_ref[...]
    ti = lax.broadcasted_iota(jnp.int32, (C, C), 0)
    si = lax.broadcasted_iota(jnp.int32, (C, C), 1)
    tril_c = jnp.where(ti >= si, 1.0, 0.0).astype(BF16)
    tril3 = jnp.concatenate([tril_c, tril_c, tril_c], axis=1)
    xr = jnp.bitwise_xor(ti, si)
    for li, hs in enumerate(HGRN_LEVELS):
        own = jnp.logical_and(lax.shift_right_logical(xr, hs.bit_length() - 1) == 1, ti > si)
        msk_scr[li] = jnp.where(own, 1.0, 0.0)
    for d in range(HGRN_DIAG):
        own = jnp.logical_and(ti - si == d, xr < HGRN_DIAG)
        msk_scr[len(HGRN_LEVELS) + d] = jnp.where(own, 1.0, 0.0)

    for c in range(tt // C):
        ffn.step()
        rows = slice(c * C, (c + 1) * C)
        q_all = p_scr[rows, 0:HGRN_K]
        f_pre = p_scr[rows, HGRN_K:2 * HGRN_K]
        iv_all = p_scr[rows, 2 * HGRN_K:2 * HGRN_K + HGRN_V]
        g_all = p_scr[rows, 2 * HGRN_K + HGRN_V:2 * HGRN_K + 2 * HGRN_V]
        f = lb + (1.0 - lb) * _sigmoid(f_pre)
        lf = jnp.log(f)
        lf_hi = lf.astype(BF16)
        lf_r = lf - lf_hi.astype(F32)
        lf_mid = lf_r.astype(BF16)
        lf_lo = (lf_r - lf_mid.astype(F32)).astype(BF16)
        b_all = _dot(tril3, jnp.concatenate([lf_hi, lf_mid, lf_lo], axis=0))
        k_all = 1.0 - f
        kpad[PAD:PAD + C, :] = k_all
        bpad[PAD:PAD + C, :] = b_all
        b_last = bpad[PAD + C - 1:PAD + C, :]
        q_inter = (q_all * jnp.exp(b_all)).astype(BF16)
        k_dec = (k_all * jnp.exp(b_last - b_all)).astype(BF16)
        dec_row = jnp.exp(b_last)
        iv_bf = iv_all.astype(BF16)
        gate_all = g_all * _sigmoid(g_all)

        lev = []
        for hs in HGRN_LEVELS:
            pivots = [blk * 2 * hs + hs - 1 for blk in range(C // (2 * hs))]
            piv = jnp.concatenate(
                [jnp.broadcast_to(bpad[PAD + p:PAD + p + 1, :], (2 * hs, HGRN_K)) for p in pivots],
                axis=0)
            e = jnp.exp(-jnp.abs(b_all - piv))
            lev.append(((q_all * e).astype(BF16), (k_all * e).astype(BF16)))
        diag = [q_all * k_all]
        for d in range(1, HGRN_DIAG):
            k_sh = kpad[pl.ds(PAD - d, C), :]
            b_sh = bpad[pl.ds(PAD - d, C), :]
            diag.append(q_all * k_sh * jnp.exp(b_all - b_sh))

        for hd in heads:
            if hd % 2 == 0:
                ffn.step()
            sl = slice(hd * HGRN_DK, (hd + 1) * HGRN_DK)
            vsl = slice(hd * HGRN_DV, (hd + 1) * HGRN_DV)
            sc = jnp.zeros((C, C), F32)
            for li in range(len(HGRN_LEVELS)):
                sc = sc + _dot_nt(lev[li][0][:, sl], lev[li][1][:, sl]) * msk_scr[li]
            for d in range(HGRN_DIAG):
                sc = sc + (jnp.sum(diag[d][:, sl], axis=-1, keepdims=True)
                           * msk_scr[len(HGRN_LEVELS) + d])
            st = st_scr[hd]
            o = _dot_nt(q_inter[:, sl], st.astype(BF16)) + _dot(sc.astype(BF16), iv_bf[:, vsl])
            st_scr[hd] = st * dec_row[:, sl] + _dot(iv_all[:, vsl].T.astype(BF16), k_dec[:, sl])
            on = _rms(o, gn_ref[:, vsl])
            mix_scr[rows, vsl] = (on * gate_all[:, vsl]).astype(BF16)

    prow = lax.broadcasted_iota(jnp.int32, (SGU_CHUNK, SGU_CHUNK), 0)
    pcol = lax.broadcasted_iota(jnp.int32, (SGU_CHUNK, SGU_CHUNK), 1)
    w_causal = [jnp.where(prow >= pcol, sw_ref[gi], 0.0).astype(BF16) for gi in range(SGU_GROUPS)]
    for n in range(tt // SGU_CHUNK):
        rows = slice(n * SGU_CHUNK, (n + 1) * SGU_CHUNK)
        for gi in range(SGU_GROUPS):
            if gi % 2 == 0:
                ffn.step()
            csl = slice(gi * SGU_CH, (gi + 1) * SGU_CH)
            u = _gelu_tanh(p_scr[rows, 2 * HGRN_K + 2 * HGRN_V + gi * SGU_CH:
                                 2 * HGRN_K + 2 * HGRN_V + (gi + 1) * SGU_CH])
            v = _gelu_tanh(p_scr[rows, 2 * HGRN_K + 2 * HGRN_V + MIX_HALF + gi * SGU_CH:
                                 2 * HGRN_K + 2 * HGRN_V + MIX_HALF + (gi + 1) * SGU_CH])
            mu = jnp.mean(v, axis=-1, keepdims=True)
            vc = v - mu
            var = jnp.mean(vc * vc, axis=-1, keepdims=True)
            vn = vc * lax.rsqrt(var + EPS) * lng_ref[:, csl] + lnb_ref[:, csl]
            z = _dot(w_causal[gi], vn.astype(BF16)) + sb_ref[gi]
            mix_scr[rows, HGRN_V + gi * SGU_CH:HGRN_V + (gi + 1) * SGU_CH] = (u * z).astype(BF16)

    xm_scr[slot] = x + _dot(mix_scr[...], wout_ref[...])
    ffn.finish()


def _tile_index_maps(nt, steps):
    def mixer_tile(s):
        m = jnp.minimum(s, steps - 1)
        return m // nt, m % nt, 0

    def ffn_tile(s):
        f = jnp.maximum(s - 1, 0)
        return f // nt, f % nt, 0

    return mixer_tile, ffn_tile


def _even_layer(x, norm_w, w_in, w_out, lb, gnorm, ln_g, ln_b, sgu_w, sgu_b_bc,
                ffn_norm_w, w_gate, w_up, w_down):
    bsz, seq, _ = x.shape
    tt = min(MIX_TILE, seq)
    nt = seq // tt
    steps = bsz * nt
    mixer_tile, ffn_tile = _tile_index_maps(nt, steps)
    return pl.pallas_call(
        functools.partial(_even_kernel, tt=tt, nt=nt, steps=steps),
        grid=(steps + 1,),
        in_specs=[
            pl.BlockSpec((1, tt, D_MODEL), mixer_tile),
            _const_spec((1, D_MODEL)),
            _const_spec((D_MODEL, IN_EVEN)),
            _const_spec((D_MODEL, D_MODEL)),
            _const_spec((1, HGRN_K)),
            _const_spec((1, HGRN_V)),
            _const_spec((1, MIX_HALF)),
            _const_spec((1, MIX_HALF)),
            _const_spec((SGU_GROUPS, SGU_CHUNK, SGU_CHUNK)),
            _const_spec((SGU_GROUPS, SGU_CHUNK, SGU_CH)),
            _const_spec((1, D_MODEL)),
            _const_spec((D_MODEL, D_FF)),
            _const_spec((D_MODEL, D_FF)),
            _const_spec((D_FF, D_MODEL)),
        ],
        out_specs=pl.BlockSpec((1, tt, D_MODEL), ffn_tile),
        out_shape=jax.ShapeDtypeStruct(x.shape, F32),
        scratch_shapes=[
            pltpu.VMEM((tt, IN_EVEN), F32),
            pltpu.VMEM((HGRN_HEADS, HGRN_DV, HGRN_DK), F32),
            pltpu.VMEM((SUBLANES + HGRN_CHUNK, HGRN_K), F32),
            pltpu.VMEM((SUBLANES + HGRN_CHUNK, HGRN_K), F32),
            pltpu.VMEM((len(HGRN_LEVELS) + HGRN_DIAG, HGRN_CHUNK, HGRN_CHUNK), F32),
            pltpu.VMEM((tt, D_MODEL), BF16),
            pltpu.VMEM((2, tt, D_MODEL), F32),
            pltpu.VMEM((tt, D_FF), BF16),
        ],
        compiler_params=pltpu.CompilerParams(
            dimension_semantics=("arbitrary",), vmem_limit_bytes=VMEM_LIMIT_BYTES),
        name="even_layer",
    )(x, norm_w, w_in, w_out, lb, gnorm, ln_g, ln_b, sgu_w, sgu_b_bc,
      ffn_norm_w, w_gate, w_up, w_down)


ODD_CQ = 2 * CONV_CH
ODD_CKV = ODD_CQ + MLA_Q_RANK
ODD_ROPE = ODD_CKV + MLA_KV_RANK
IN_ODD_EXT = ODD_ROPE + 2 * MLA_ROPE
Q_SLOT = MLA_NOPE + 2 * MLA_ROPE


def _odd_kernel(x_ref, pos_ref, nw_ref, win_ref, wout_ref, cw_ref, cb_ref, clg_ref, clb_ref,
                qn_ref, wuq_ref, kvn_ref, wukv_ref, inv_ref, sgn_ref, o_ref,
                p_scr, hpad, hshift, k_scr, vt_scr, q_scr, m_scr, l_scr, acc_scr, s_scr, mix_scr,
                *, tt):
    t_idx = pl.program_id(1)
    scale = (MLA_NOPE + MLA_ROPE) ** -0.5

    @pl.when(t_idx == 0)
    def _():
        hpad[0:CONV_HALO, :] = jnp.zeros((CONV_HALO, CONV_CH), F32)

    x = x_ref[0]
    h = _rms(x, nw_ref[...]).astype(BF16)
    p_scr[...] = _dot(h, win_ref[...])

    a = p_scr[:, 0:CONV_CH]
    gate = p_scr[:, CONV_CH:2 * CONV_CH]
    hpad[CONV_HALO:CONV_HALO + tt, :] = a * _sigmoid(gate)
    for r in range(1, SUBLANES):
        hshift[r - 1] = hpad[pl.ds(r, tt + CONV_HALO - SUBLANES), :]
    acc = jnp.zeros((tt, CONV_CH), F32) + cb_ref[...]
    for w in range(CONV_WIDTH):
        r = (CONV_OFF + w) % SUBLANES
        a8 = (CONV_OFF + w) - r
        tap = hpad[a8:a8 + tt, :] if r == 0 else hshift[r - 1, a8:a8 + tt, :]
        acc = acc + tap * cw_ref[w:w + 1, :]
    hpad[0:CONV_HALO, :] = hpad[tt:tt + CONV_HALO, :]
    mu = jnp.mean(acc, axis=-1, keepdims=True)
    ac = acc - mu
    var = jnp.mean(ac * ac, axis=-1, keepdims=True)
    cn = ac * lax.rsqrt(var + EPS) * clg_ref[...] + clb_ref[...]
    mix_scr[:, 0:CONV_CH] = (cn * _sigmoid(cn)).astype(BF16)

    cq = _rms(p_scr[:, ODD_CQ:ODD_CKV], qn_ref[...]).astype(BF16)
    ckv = _rms(p_scr[:, ODD_CKV:ODD_ROPE], kvn_ref[...]).astype(BF16)
    qf = _dot(cq, wuq_ref[...])
    kvf = _dot(ckv, wukv_ref[...])

    ang = pos_ref[0] * inv_ref[...]
    cos = jnp.cos(ang)
    sin_s = jnp.sin(ang) * sgn_ref[...]
    rot = jnp.concatenate([cos[:, :MLA_ROPE], sin_s[:, MLA_ROPE:]], axis=1)
    kr = p_scr[:, ODD_ROPE:IN_ODD_EXT] * rot
    kr = kr + pltpu.roll(kr, MLA_ROPE, axis=1)
    q_mult = jnp.concatenate([jnp.full((tt, MLA_NOPE), scale, F32), rot * scale], axis=1)

    r0 = pl.multiple_of(t_idx * tt, tt)
    for hd in range(MLA_HEADS):
        q_scr[hd] = (qf[:, hd * Q_SLOT:(hd + 1) * Q_SLOT] * q_mult).astype(BF16)
        kv0 = hd * (MLA_NOPE + MLA_V)
        k_scr[hd, pl.ds(r0, tt), 0:MLA_NOPE] = kvf[:, kv0:kv0 + MLA_NOPE].astype(BF16)
        k_scr[hd, pl.ds(r0, tt), MLA_NOPE:Q_SLOT] = kr.astype(BF16)
        vt_scr[hd, t_idx] = kvf[:, kv0 + MLA_NOPE:kv0 + MLA_NOPE + MLA_V].T.astype(BF16)
        m_scr[hd] = jnp.full((1, tt), NEG_BIG, F32)
        l_scr[hd] = jnp.zeros((1, tt), F32)
        acc_scr[hd] = jnp.zeros((MLA_V, tt), F32)

    heads = range(MLA_HEADS)

    def scores_into(kb, slot):
        k0 = pl.multiple_of(kb * tt, tt)
        for hd in heads:
            s_scr[slot, hd] = _dot_nt(k_scr[hd, pl.ds(k0, tt), :], q_scr[hd])

    def attend(kb, masked):
        sts = [s_scr[kb % 2, hd] for hd in heads]
        if not masked:
            scores_into(kb + 1, (kb + 1) % 2)
        if masked:
            krow = lax.broadcasted_iota(jnp.int32, (tt, tt), 0)
            qcol = lax.broadcasted_iota(jnp.int32, (tt, tt), 1)
            sts = [jnp.where(krow <= qcol, st, NEG_BIG) for st in sts]
        m_olds = [m_scr[hd] for hd in heads]
        m_news = [jnp.maximum(m_olds[hd], jnp.max(sts[hd], axis=0, keepdims=True)) for hd in heads]
        alphas = [jnp.exp(m_olds[hd] - m_news[hd]) for hd in heads]
        prs = [jnp.exp(sts[hd] - m_news[hd]) for hd in heads]
        for hd in heads:
            l_scr[hd] = alphas[hd] * l_scr[hd] + jnp.sum(prs[hd], axis=0, keepdims=True)
            m_scr[hd] = m_news[hd]
        pvs = [_dot(vt_scr[hd, kb], prs[hd].astype(BF16)) for hd in heads]
        for hd in heads:
            acc_scr[hd] = acc_scr[hd] * alphas[hd] + pvs[hd]

    def kv_body(kb, carry):
        attend(kb, False)
        return carry

    scores_into(0, 0)
    lax.fori_loop(0, t_idx, kv_body, 0)
    attend(t_idx, True)
    for hd in range(MLA_HEADS):
        o_t = acc_scr[hd] * (1.0 / l_scr[hd])
        mix_scr[:, CONV_CH + hd * MLA_V:CONV_CH + (hd + 1) * MLA_V] = o_t.T.astype(BF16)

    o_ref[0] = x + _dot(mix_scr[...], wout_ref[...])


def _odd_mixer(x, pos_bc, norm_w, w_in_ext, w_out, conv_w, conv_b, cln_g, cln_b,
               q_norm, w_uq_ext, kv_norm, w_ukv, inv_row, sgn_row):
    bsz, seq, _ = x.shape
    tt = min(ODD_TILE, seq)
    return pl.pallas_call(
        functools.partial(_odd_kernel, tt=tt),
        grid=(bsz, seq // tt),
        in_specs=[
            pl.BlockSpec((1, tt, D_MODEL), lambda b, t: (b, t, 0)),
            pl.BlockSpec((1, tt, 2 * MLA_ROPE), lambda b, t: (b, t, 0)),
            _const_spec((1, D_MODEL)),
            _const_spec((D_MODEL, IN_ODD_EXT)),
            _const_spec((D_MODEL, D_MODEL)),
            _const_spec((CONV_WIDTH, CONV_CH)),
            _const_spec((1, CONV_CH)),
            _const_spec((1, CONV_CH)),
            _const_spec((1, CONV_CH)),
            _const_spec((1, MLA_Q_RANK)),
            _const_spec((MLA_Q_RANK, MLA_HEADS * Q_SLOT)),
            _const_spec((1, MLA_KV_RANK)),
            _const_spec((MLA_KV_RANK, MLA_HEADS * (MLA_NOPE + MLA_V))),
            _const_spec((1, 2 * MLA_ROPE)),
            _const_spec((1, 2 * MLA_ROPE)),
        ],
        out_specs=pl.BlockSpec((1, tt, D_MODEL), lambda b, t: (b, t, 0)),
        out_shape=jax.ShapeDtypeStruct(x.shape, F32),
        scratch_shapes=[
            pltpu.VMEM((tt, IN_ODD_EXT), F32),
            pltpu.VMEM((CONV_HALO + tt, CONV_CH), F32),
            pltpu.VMEM((SUBLANES - 1, tt + CONV_HALO - SUBLANES, CONV_CH), F32),
            pltpu.VMEM((MLA_HEADS, seq, Q_SLOT), BF16),
            pltpu.VMEM((MLA_HEADS, seq // tt, MLA_V, tt), BF16),
            pltpu.VMEM((MLA_HEADS, tt, Q_SLOT), BF16),
            pltpu.VMEM((MLA_HEADS, 1, tt), F32),
            pltpu.VMEM((MLA_HEADS, 1, tt), F32),
            pltpu.VMEM((MLA_HEADS, MLA_V, tt), F32),
            pltpu.VMEM((2, MLA_HEADS, tt, tt), F32),
            pltpu.VMEM((tt, D_MODEL), BF16),
        ],
        compiler_params=pltpu.CompilerParams(
            dimension_semantics=("arbitrary", "arbitrary"), vmem_limit_bytes=VMEM_LIMIT_BYTES),
        name="odd_mixer",
    )(x, pos_bc, norm_w, w_in_ext, w_out, conv_w, conv_b, cln_g, cln_b,
      q_norm, w_uq_ext, kv_norm, w_ukv, inv_row, sgn_row)


def _swap_halves(w):
    half = w.shape[-1] // 2
    return jnp.concatenate([w[..., half:], w[..., :half]], axis=-1)


def kernel(x, positions, mix_norm, ffn_norm, ffn_gate, ffn_up, ffn_down, w_in_even, w_out_even,
           hgrn_lb_logits, hgrn_gnorm, sgu_ln_g, sgu_ln_b, sgu_w, sgu_b, w_in_odd, w_out_odd,
           conv_w, conv_b, conv_ln_g, conv_ln_b, mla_q_norm, mla_w_uq, mla_kv_norm, mla_w_ukv,
           final_norm):
    bsz, seq, _ = x.shape
    depth = mix_norm.shape[0]
    lower_bounds = jnp.cumsum(jax.nn.softmax(hgrn_lb_logits.astype(F32), axis=0), axis=0)

    inv = 1.0 / (ROPE_THETA ** (jnp.arange(0, MLA_ROPE, 2, dtype=F32) / MLA_ROPE))
    inv_row = jnp.tile(inv, 4)[None, :]
    sgn_row = jnp.concatenate([jnp.ones((MLA_ROPE,), F32), -jnp.ones((MLA_ROPE // 2,), F32),
                               jnp.ones((MLA_ROPE // 2,), F32)])[None, :]
    pos_bc = jnp.broadcast_to(positions.astype(F32)[:, :, None], (bsz, seq, 2 * MLA_ROPE))

    row = lambda v: v.reshape(1, -1).astype(F32)
    for layer in range(depth):
        j = layer // 2
        if layer % 2 == 0:
            sgu_b_bc = jnp.broadcast_to(sgu_b[j][:, :, None], (SGU_GROUPS, SGU_CHUNK, SGU_CH))
            x = _even_layer(x, row(mix_norm[layer]), w_in_even[j].astype(BF16),
                            w_out_even[j].astype(BF16), row(lower_bounds[j]), row(hgrn_gnorm[j]),
                            row(sgu_ln_g[j]), row(sgu_ln_b[j]), sgu_w[j], sgu_b_bc,
                            row(ffn_norm[layer]), ffn_gate[layer].astype(BF16),
                            ffn_up[layer].astype(BF16), ffn_down[layer].astype(BF16))
            continue
        else:
            w_in = w_in_odd[j]
            w_in_ext = jnp.concatenate([w_in, _swap_halves(w_in[:, ODD_ROPE:])], axis=1).astype(BF16)
            wq = mla_w_uq[j].reshape(MLA_Q_RANK, MLA_HEADS, MLA_NOPE + MLA_ROPE)
            wq_ext = jnp.concatenate([wq, _swap_halves(wq[:, :, MLA_NOPE:])], axis=2)
            wq_ext = wq_ext.reshape(MLA_Q_RANK, MLA_HEADS * Q_SLOT).astype(BF16)
            x = _odd_mixer(x, pos_bc, row(mix_norm[layer]), w_in_ext, w_out_odd[j].astype(BF16),
                           conv_w[j], row(conv_b[j]), row(conv_ln_g[j]), row(conv_ln_b[j]),
                           row(mla_q_norm[j]), wq_ext, row(mla_kv_norm[j]),
                           mla_w_ukv[j].astype(BF16), inv_row, sgn_row)
        x2 = _ffn(x.reshape(bsz * seq, D_MODEL), row(ffn_norm[layer]), ffn_gate[layer].astype(BF16),
                  ffn_up[layer].astype(BF16), ffn_down[layer].astype(BF16), row(final_norm),
                  final=(layer == depth - 1))
        x = x2.reshape(bsz, seq, D_MODEL)
    return x
```

```python
import functools
import math

import jax
import jax.numpy as jnp
from jax import lax
from jax.experimental import pallas as pl
from jax.experimental.pallas import tpu as pltpu

F32 = jnp.float32
BF16 = jnp.bfloat16

D_MODEL = 1024
MIX_HALF = D_MODEL // 2
HGRN_HEADS = 4
HGRN_DK = 128
HGRN_DV = MIX_HALF // HGRN_HEADS
HGRN_K = HGRN_HEADS * HGRN_DK
HGRN_V = HGRN_HEADS * HGRN_DV
SGU_GROUPS = 4
SGU_CH = MIX_HALF // SGU_GROUPS
SGU_CHUNK = 128
CONV_CH = MIX_HALF
CONV_WIDTH = 31
MLA_HEADS = 4
MLA_NOPE = 128
MLA_ROPE = 64
MLA_V = 128
MLA_Q_RANK = 384
MLA_KV_RANK = 256
ROPE_THETA = 10000.0
D_FF = -(-8 * D_MODEL // (3 * 256)) * 256
EPS = 1e-6
IN_EVEN = 2 * HGRN_K + 2 * HGRN_V + 2 * MIX_HALF
IN_ODD = 2 * CONV_CH + MLA_Q_RANK + MLA_KV_RANK + MLA_ROPE

LANES = 128
SUBLANES = 8
VMEM_LIMIT_BYTES = 56 * 1024 * 1024

MIX_TILE = 256
ODD_TILE = 512
FFN_TILE = 512
FFN_CHUNKS = ((0, 768), (768, 1536), (1536, 2304), (2304, D_FF))
FFN_UP_BLOCK = 512
FFN_DOWN_BLOCK = 512
HGRN_CHUNK = 64
HGRN_LEVELS = (32, 16, 8, 4)
HGRN_DIAG = 4
CONV_HALO = 32
CONV_OFF = CONV_HALO - (CONV_WIDTH - 1)
CONV_ROWS = 32
NEG_BIG = -1e30


def _rms(x, w):
    return x * lax.rsqrt(jnp.mean(x * x, axis=-1, keepdims=True) + EPS) * w


def _dot(a, b):
    return jnp.dot(a, b, preferred_element_type=F32)


def _dot_nt(a, b):
    return lax.dot_general(a, b, (((1,), (1,)), ((), ())), preferred_element_type=F32)


def _sigmoid(x):
    return 1.0 / (1.0 + jnp.exp(-x))


def _gelu_tanh(x):
    return 0.5 * x * (1.0 + jnp.tanh(math.sqrt(2.0 / math.pi) * (x + 0.044715 * (x * x * x))))


def _const_spec(shape):
    nd = len(shape)
    return pl.BlockSpec(shape, lambda *_: (0,) * nd, pipeline_mode=pl.Buffered(1))


class _TrailingFfn:
    def __init__(self, x_prev, nw_ref, wg_ref, wu_ref, wd_ref, act_scr, out_fn):
        self.x, self.wg_ref, self.wu_ref, self.wd_ref = x_prev, wg_ref, wu_ref, wd_ref
        self.act_scr, self.out_fn = act_scr, out_fn
        self.h = _rms(x_prev, nw_ref[...]).astype(BF16)
        up = [(self._up, c0) for c0 in range(0, D_FF, FFN_UP_BLOCK)]
        down = [(self._down, n0) for n0 in range(0, D_MODEL, FFN_DOWN_BLOCK)]
        self.todo = up + down

    def _up(self, c0):
        cols = slice(c0, min(c0 + FFN_UP_BLOCK, D_FF))
        g = _dot(self.h, self.wg_ref[:, cols])
        u = _dot(self.h, self.wu_ref[:, cols])
        self.act_scr[:, cols] = (g * _sigmoid(g) * u).astype(BF16)

    def _down(self, n0):
        cols = slice(n0, n0 + FFN_DOWN_BLOCK)
        self.out_fn(cols, self.x[:, cols] + _dot(self.act_scr[...], self.wd_ref[:, cols]))

    def step(self, n=1):
        for _ in range(min(n, len(self.todo))):
            fn, arg = self.todo.pop(0)
            fn(arg)

    def finish(self):
        self.step(len(self.todo))


def _ffn_kernel(x_ref, nw_ref, wg_ref, wu_ref, wd_ref, fw_ref, o_ref, *, final):
    x = x_ref[...]
    h = _rms(x, nw_ref[...]).astype(BF16)
    acc = x
    for c0, c1 in FFN_CHUNKS:
        g = _dot(h, wg_ref[:, c0:c1])
        u = _dot(h, wu_ref[:, c0:c1])
        a = (g * _sigmoid(g) * u).astype(BF16)
        acc = acc + _dot(a, wd_ref[c0:c1, :])
    if final:
        acc = _rms(acc, fw_ref[...])
    o_ref[...] = acc


def _ffn(x2, norm_w, w_gate, w_up, w_down, final_w, *, final):
    n = x2.shape[0]
    tm = min(FFN_TILE, n)
    return pl.pallas_call(
        functools.partial(_ffn_kernel, final=final),
        grid=(n // tm,),
        in_specs=[
            pl.BlockSpec((tm, D_MODEL), lambda i: (i, 0)),
            _const_spec((1, D_MODEL)),
            _const_spec((D_MODEL, D_FF)),
            _const_spec((D_MODEL, D_FF)),
            _const_spec((D_FF, D_MODEL)),
            _const_spec((1, D_MODEL)),
        ],
        out_specs=pl.BlockSpec((tm, D_MODEL), lambda i: (i, 0)),
        out_shape=jax.ShapeDtypeStruct((n, D_MODEL), F32),
        compiler_params=pltpu.CompilerParams(
            dimension_semantics=("arbitrary",), vmem_limit_bytes=VMEM_LIMIT_BYTES),
        name="ffn_final" if final else "ffn",
    )(x2, norm_w, w_gate, w_up, w_down, final_w)


def _even_kernel(x_ref, nw_ref, win_ref, wout_ref, lb_ref, gn_ref, lng_ref, lnb_ref,
                 sw_ref, sb_ref, fnw_ref, wg_ref, wu_ref, wd_ref, o_ref,
                 p_scr, st_scr, kpad, bpad, msk_scr, mix_scr, xm_scr, act_scr, *, tt, nt, steps):
    s_idx = pl.program_id(0)
    t_idx = lax.rem(jnp.minimum(s_idx, steps - 1), nt)
    slot = lax.rem(s_idx, 2)
    C, PAD = HGRN_CHUNK, SUBLANES
    heads = range(HGRN_HEADS)

    @pl.when(s_idx == 0)
    def _():
        xm_scr[1] = jnp.zeros((tt, D_MODEL), F32)

    @pl.when(t_idx == 0)
    def _():
        st_scr[...] = jnp.zeros_like(st_scr)
        kpad[:, 0:PAD, :] = jnp.zeros((HGRN_HEADS, PAD, HGRN_DK), F32)
        bpad[:, 0:PAD, :] = jnp.zeros((HGRN_HEADS, PAD, HGRN_DK), F32)

    def ffn_out(cols, val):
        o_ref[0, :, cols] = val

    ffn = _TrailingFfn(xm_scr[1 - slot], fnw_ref, wg_ref, wu_ref, wd_ref, act_scr, ffn_out)
    x = x_ref[0]
    h = _rms(x, nw_ref[...]).astype(BF16)
    p_scr[...] = _dot(h, win_ref[...])

    lb = lb_ref[...]
    ti = lax.broadcasted_iota(jnp.int32, (C, C), 0)
    si = lax.broadcasted_iota(jnp.int32, (C, C), 1)
    tril_c = jnp.where(ti >= si, 1.0, 0.0).astype(BF16)
    tril3 = jnp.concatenate([tril_c, tril_c, tril_c], axis=1)
    xr = jnp.bitwise_xor(ti, si)
    for li, hs in enumerate(HGRN_LEVELS):
        own = jnp.logical_and(lax.shift_right_logical(xr, hs.bit_length() - 1) == 1, ti > si)
        msk_scr[li] = jnp.where(own, 1.0, 0.0)
    for d in range(HGRN_DIAG):
        own = jnp.logical_and(ti - si == d, xr < HGRN_DIAG)
        msk_scr[len(HGRN_LEVELS) + d] = jnp.where(own, 1.0, 0.0)

    for c in range(tt // C):
        ffn.step()
        rows = slice(c * C, (c + 1) * C)
        q_all = p_scr[rows, 0:HGRN_K]
        f_pre = p_scr[rows, HGRN_K:2 * HGRN_K]
        iv_all = p_scr[rows, 2 * HGRN_K:2 * HGRN_K + HGRN_V]
        g_all = p_scr[rows, 2 * HGRN_K + HGRN_V:2 * HGRN_K + 2 * HGRN_V]
        f = lb + (1.0 - lb) * _sigmoid(f_pre)
        lf = jnp.log(f)
        lf_hi = lf.astype(BF16)
        lf_r = lf - lf_hi.astype(F32)
        lf_mid = lf_r.astype(BF16)
        lf_lo = (lf_r - lf_mid.astype(F32)).astype(BF16)
        b_all = _dot(tril3, jnp.concatenate([lf_hi, lf_mid, lf_lo], axis=0))
        k_all = 1.0 - f
        for hd in heads:
            kpad[hd, PAD:PAD + C, :] = k_all[:, hd * HGRN_DK:(hd + 1) * HGRN_DK]
            bpad[hd, PAD:PAD + C, :] = b_all[:, hd * HGRN_DK:(hd + 1) * HGRN_DK]

        def rows_of(ref, start, size):
            return jnp.concatenate([ref[hd, start:start + size, :] for hd in heads], axis=1)

        b_last = rows_of(bpad, PAD + C - 1, 1)
        q_inter = (q_all * jnp.exp(b_all)).astype(BF16)
        k_dec = (k_all * jnp.exp(b_last - b_all)).astype(BF16)
        dec_row = jnp.exp(b_last)
        iv_bf = iv_all.astype(BF16)
        gate_all = g_all * _sigmoid(g_all)

        lev = []
        for hs in HGRN_LEVELS:
            pivots = [blk * 2 * hs + hs - 1 for blk in range(C // (2 * hs))]
            piv = jnp.concatenate(
                [jnp.broadcast_to(rows_of(bpad, PAD + p, 1), (2 * hs, HGRN_K)) for p in pivots],
                axis=0)
            e = jnp.exp(-jnp.abs(b_all - piv))
            lev.append(((q_all * e).astype(BF16), (k_all * e).astype(BF16)))
        diag = [q_all * k_all]
        for d in range(1, HGRN_DIAG):
            k_sh = rows_of(kpad, PAD - d, C)
            b_sh = rows_of(bpad, PAD - d, C)
            diag.append(q_all * k_sh * jnp.exp(b_all - b_sh))

        for hd in heads:
            if hd % 2 == 0:
                ffn.step()
            sl = slice(hd * HGRN_DK, (hd + 1) * HGRN_DK)
            vsl = slice(hd * HGRN_DV, (hd + 1) * HGRN_DV)
            sc = jnp.zeros((C, C), F32)
            for li in range(len(HGRN_LEVELS)):
                sc = sc + _dot_nt(lev[li][0][:, sl], lev[li][1][:, sl]) * msk_scr[li]
            for d in range(HGRN_DIAG):
                sc = sc + (jnp.sum(diag[d][:, sl], axis=-1, keepdims=True)
                           * msk_scr[len(HGRN_LEVELS) + d])
            st = st_scr[hd]
            o = _dot_nt(q_inter[:, sl], st.astype(BF16)) + _dot(sc.astype(BF16), iv_bf[:, vsl])
            st_scr[hd] = st * dec_row[:, sl] + _dot(iv_all[:, vsl].T.astype(BF16), k_dec[:, sl])
            on = _rms(o, gn_ref[:, vsl])
            mix_scr[rows, vsl] = (on * gate_all[:, vsl]).astype(BF16)

    prow = lax.broadcasted_iota(jnp.int32, (SGU_CHUNK, SGU_CHUNK), 0)
    pcol = lax.broadcasted_iota(jnp.int32, (SGU_CHUNK, SGU_CHUNK), 1)
    w_causal = [jnp.where(prow >= pcol, sw_ref[gi], 0.0).astype(BF16) for gi in range(SGU_GROUPS)]
    for n in range(tt // SGU_CHUNK):
        rows = slice(n * SGU_CHUNK, (n + 1) * SGU_CHUNK)
        for gi in range(SGU_GROUPS):
            if gi % 2 == 0:
                ffn.step()
            csl = slice(gi * SGU_CH, (gi + 1) * SGU_CH)
            u = _gelu_tanh(p_scr[rows, 2 * HGRN_K + 2 * HGRN_V + gi * SGU_CH:
                                 2 * HGRN_K + 2 * HGRN_V + (gi + 1) * SGU_CH])
            v = _gelu_tanh(p_scr[rows, 2 * HGRN_K + 2 * HGRN_V + MIX_HALF + gi * SGU_CH:
                                 2 * HGRN_K + 2 * HGRN_V + MIX_HALF + (gi + 1) * SGU_CH])
            mu = jnp.mean(v, axis=-1, keepdims=True)
            vc = v - mu
            var = jnp.mean(vc * vc, axis=-1, keepdims=True)
            vn = vc * lax.rsqrt(var + EPS) * lng_ref[:, csl] + lnb_ref[:, csl]
            z = _dot(w_causal[gi], vn.astype(BF16)) + sb_ref[gi]
            mix_scr[rows, HGRN_V + gi * SGU_CH:HGRN_V + (gi + 1) * SGU_CH] = (u * z).astype(BF16)

    xm_scr[slot] = x + _dot(mix_scr[...], wout_ref[...])
    ffn.finish()


def _tile_index_maps(nt, steps):
    def mixer_tile(s):
        m = jnp.minimum(s, steps - 1)
        return m // nt, m % nt, 0

    def ffn_tile(s):
        f = jnp.maximum(s - 1, 0)
        return f // nt, f % nt, 0

    return mixer_tile, ffn_tile


def _even_layer(x, norm_w, w_in, w_out, lb, gnorm, ln_g, ln_b, sgu_w, sgu_b_bc,
                ffn_norm_w, w_gate, w_up, w_down):
    bsz, seq, _ = x.shape
    tt = min(MIX_TILE, seq)
    nt = seq // tt
    steps = bsz * nt
    mixer_tile, ffn_tile = _tile_index_maps(nt, steps)
    return pl.pallas_call(
        functools.partial(_even_kernel, tt=tt, nt=nt, steps=steps),
        grid=(steps + 1,),
        in_specs=[
            pl.BlockSpec((1, tt, D_MODEL), mixer_tile),
            _const_spec((1, D_MODEL)),
            _const_spec((D_MODEL, IN_EVEN)),
            _const_spec((D_MODEL, D_MODEL)),
            _const_spec((1, HGRN_K)),
            _const_spec((1, HGRN_V)),
            _const_spec((1, MIX_HALF)),
            _const_spec((1, MIX_HALF)),
            _const_spec((SGU_GROUPS, SGU_CHUNK, SGU_CHUNK)),
            _const_spec((SGU_GROUPS, SGU_CHUNK, SGU_CH)),
            _const_spec((1, D_MODEL)),
            _const_spec((D_MODEL, D_FF)),
            _const_spec((D_MODEL, D_FF)),
            _const_spec((D_FF, D_MODEL)),
        ],
        out_specs=pl.BlockSpec((1, tt, D_MODEL), ffn_tile),
        out_shape=jax.ShapeDtypeStruct(x.shape, F32),
        scratch_shapes=[
            pltpu.VMEM((tt, IN_EVEN), F32),
            pltpu.VMEM((HGRN_HEADS, HGRN_DV, HGRN_DK), F32),
            pltpu.VMEM((HGRN_HEADS, SUBLANES + HGRN_CHUNK, HGRN_DK), F32),
            pltpu.VMEM((HGRN_HEADS, SUBLANES + HGRN_CHUNK, HGRN_DK), F32),
            pltpu.VMEM((len(HGRN_LEVELS) + HGRN_DIAG, HGRN_CHUNK, HGRN_CHUNK), F32),
            pltpu.VMEM((tt, D_MODEL), BF16),
            pltpu.VMEM((2, tt, D_MODEL), F32),
            pltpu.VMEM((tt, D_FF), BF16),
        ],
        compiler_params=pltpu.CompilerParams(
            dimension_semantics=("arbitrary",), vmem_limit_bytes=VMEM_LIMIT_BYTES),
        name="even_layer",
    )(x, norm_w, w_in, w_out, lb, gnorm, ln_g, ln_b, sgu_w, sgu_b_bc,
      ffn_norm_w, w_gate, w_up, w_down)


ODD_CQ = 2 * CONV_CH
ODD_CKV = ODD_CQ + MLA_Q_RANK
ODD_ROPE = ODD_CKV + MLA_KV_RANK
IN_ODD_EXT = ODD_ROPE + 2 * MLA_ROPE
Q_SLOT = MLA_NOPE + 2 * MLA_ROPE


def _odd_kernel(x_ref, pos_ref, nw_ref, win_ref, wout_ref, cw_ref, cb_ref, clg_ref, clb_ref,
                qn_ref, wuq_ref, kvn_ref, wukv_ref, inv_ref, sgn_ref, o_ref,
                p_scr, hpad, cacc_scr, k_scr, vt_scr, q_scr, m_scr, l_scr, acc_scr, s_scr, mix_scr,
                *, tt):
    t_idx = pl.program_id(1)
    scale = (MLA_NOPE + MLA_ROPE) ** -0.5

    @pl.when(t_idx == 0)
    def _():
        hpad[:, 0:CONV_HALO, :] = jnp.zeros((CONV_CH // LANES, CONV_HALO, LANES), F32)

    x = x_ref[0]
    h = _rms(x, nw_ref[...]).astype(BF16)
    p_scr[...] = _dot(h, win_ref[...])

    for cb in range(CONV_CH // LANES):
        cs = slice(cb * LANES, (cb + 1) * LANES)
        a = p_scr[:, cs]
        gate = p_scr[:, CONV_CH + cb * LANES:CONV_CH + (cb + 1) * LANES]
        hpad[cb, CONV_HALO:CONV_HALO + tt, :] = a * _sigmoid(gate)
    for r in range(0, tt, CONV_ROWS):
        for cb in range(CONV_CH // LANES):
            cs = slice(cb * LANES, (cb + 1) * LANES)
            acc = jnp.broadcast_to(cb_ref[:, cs], (CONV_ROWS, LANES))
            for w in range(CONV_WIDTH):
                r0w = r + CONV_OFF + w
                acc = acc + hpad[cb, r0w:r0w + CONV_ROWS, :] * cw_ref[w:w + 1, cs]
            cacc_scr[r:r + CONV_ROWS, cs] = acc
    for cb in range(CONV_CH // LANES):
        hpad[cb, 0:CONV_HALO, :] = hpad[cb, tt:tt + CONV_HALO, :]
    acc = cacc_scr[...]
    mu = jnp.mean(acc, axis=-1, keepdims=True)
    ac = acc - mu
    var = jnp.mean(ac * ac, axis=-1, keepdims=True)
    cn = ac * lax.rsqrt(var + EPS) * clg_ref[...] + clb_ref[...]
    mix_scr[:, 0:CONV_CH] = (cn * _sigmoid(cn)).astype(BF16)

    cq = _rms(p_scr[:, ODD_CQ:ODD_CKV], qn_ref[...]).astype(BF16)
    ckv = _rms(p_scr[:, ODD_CKV:ODD_ROPE], kvn_ref[...]).astype(BF16)
    qf = _dot(cq, wuq_ref[...])
    kvf = _dot(ckv, wukv_ref[...])

    ang = pos_ref[0] * inv_ref[...]
    cos = jnp.cos(ang)
    sin_s = jnp.sin(ang) * sgn_ref[...]
    rot = jnp.concatenate([cos[:, :MLA_ROPE], sin_s[:, MLA_ROPE:]], axis=1)
    kr = p_scr[:, ODD_ROPE:IN_ODD_EXT] * rot
    kr = kr + pltpu.roll(kr, MLA_ROPE, axis=1)
    q_mult = jnp.concatenate([jnp.full((tt, MLA_NOPE), scale, F32), rot * scale], axis=1)

    r0 = pl.multiple_of(t_idx * tt, tt)
    for hd in range(MLA_HEADS):
        q_scr[hd] = (qf[:, hd * Q_SLOT:(hd + 1) * Q_SLOT] * q_mult).astype(BF16)
        kv0 = hd * (MLA_NOPE + MLA_V)
        k_scr[hd, pl.ds(r0, tt), 0:MLA_NOPE] = kvf[:, kv0:kv0 + MLA_NOPE].astype(BF16)
        k_scr[hd, pl.ds(r0, tt), MLA_NOPE:Q_SLOT] = kr.astype(BF16)
        vt_scr[hd, t_idx] = kvf[:, kv0 + MLA_NOPE:kv0 + MLA_NOPE + MLA_V].T.astype(BF16)
        m_scr[hd] = jnp.full((1, tt), NEG_BIG, F32)
        l_scr[hd] = jnp.zeros((1, tt), F32)
        acc_scr[hd] = jnp.zeros((MLA_V, tt), F32)

    heads = range(MLA_HEADS)

    def scores_into(kb, slot):
        k0 = pl.multiple_of(kb * tt, tt)
        for hd in heads:
            s_scr[slot, hd] = _dot_nt(k_scr[hd, pl.ds(k0, tt), :], q_scr[hd])

    def attend(kb, masked):
        sts = [s_scr[kb % 2, hd] for hd in heads]
        if not masked:
            scores_into(kb + 1, (kb + 1) % 2)
        if masked:
            krow = lax.broadcasted_iota(jnp.int32, (tt, tt), 0)
            qcol = lax.broadcasted_iota(jnp.int32, (tt, tt), 1)
            sts = [jnp.where(krow <= qcol, st, NEG_BIG) for st in sts]
        m_olds = [m_scr[hd] for hd in heads]
        m_news = [jnp.maximum(m_olds[hd], jnp.max(sts[hd], axis=0, keepdims=True)) for hd in heads]
        alphas = [jnp.exp(m_olds[hd] - m_news[hd]) for hd in heads]
        prs = [jnp.exp(sts[hd] - m_news[hd]) for hd in heads]
        for hd in heads:
            l_scr[hd] = alphas[hd] * l_scr[hd] + jnp.sum(prs[hd], axis=0, keepdims=True)
            m_scr[hd] = m_news[hd]
        pvs = [_dot(vt_scr[hd, kb], prs[hd].astype(BF16)) for hd in heads]
        for hd in heads:
            acc_scr[hd] = acc_scr[hd] * alphas[hd] + pvs[hd]

    def kv_body(kb, carry):
        attend(kb, False)
        return carry

    scores_into(0, 0)
    lax.fori_loop(0, t_idx, kv_body, 0)
    attend(t_idx, True)
    for hd in range(MLA_HEADS):
        o_t = acc_scr[hd] * (1.0 / l_scr[hd])
        mix_scr[:, CONV_CH + hd * MLA_V:CONV_CH + (hd + 1) * MLA_V] = o_t.T.astype(BF16)

    o_ref[0] = x + _dot(mix_scr[...], wout_ref[...])


def _odd_mixer(x, pos_bc, norm_w, w_in_ext, w_out, conv_w, conv_b, cln_g, cln_b,
               q_norm, w_uq_ext, kv_norm, w_ukv, inv_row, sgn_row):
    bsz, seq, _ = x.shape
    tt = min(ODD_TILE, seq)
    return pl.pallas_call(
        functools.partial(_odd_kernel, tt=tt),
        grid=(bsz, seq // tt),
        in_specs=[
            pl.BlockSpec((1, tt, D_MODEL), lambda b, t: (b, t, 0)),
            pl.BlockSpec((1, tt, 2 * MLA_ROPE), lambda b, t: (b, t, 0)),
            _const_spec((1, D_MODEL)),
            _const_spec((D_MODEL, IN_ODD_EXT)),
            _const_spec((D_MODEL, D_MODEL)),
            _const_spec((CONV_WIDTH, CONV_CH)),
            _const_spec((1, CONV_CH)),
            _const_spec((1, CONV_CH)),
            _const_spec((1, CONV_CH)),
            _const_spec((1, MLA_Q_RANK)),
            _const_spec((MLA_Q_RANK, MLA_HEADS * Q_SLOT)),
            _const_spec((1, MLA_KV_RANK)),
            _const_spec((MLA_KV_RANK, MLA_HEADS * (MLA_NOPE + MLA_V))),
            _const_spec((1, 2 * MLA_ROPE)),
            _const_spec((1, 2 * MLA_ROPE)),
        ],
        out_specs=pl.BlockSpec((1, tt, D_MODEL), lambda b, t: (b, t, 0)),
        out_shape=jax.ShapeDtypeStruct(x.shape, F32),
        scratch_shapes=[
            pltpu.VMEM((tt, IN_ODD_EXT), F32),
            pltpu.VMEM((CONV_CH // LANES, CONV_HALO + tt, LANES), F32),
            pltpu.VMEM((tt, CONV_CH), F32),
            pltpu.VMEM((MLA_HEADS, seq, Q_SLOT), BF16),
            pltpu.VMEM((MLA_HEADS, seq // tt, MLA_V, tt), BF16),
            pltpu.VMEM((MLA_HEADS, tt, Q_SLOT), BF16),
            pltpu.VMEM((MLA_HEADS, 1, tt), F32),
            pltpu.VMEM((MLA_HEADS, 1, tt), F32),
            pltpu.VMEM((MLA_HEADS, MLA_V, tt), F32),
            pltpu.VMEM((2, MLA_HEADS, tt, tt), F32),
            pltpu.VMEM((tt, D_MODEL), BF16),
        ],
        compiler_params=pltpu.CompilerParams(
            dimension_semantics=("arbitrary", "arbitrary"), vmem_limit_bytes=VMEM_LIMIT_BYTES),
        name="odd_mixer",
    )(x, pos_bc, norm_w, w_in_ext, w_out, conv_w, conv_b, cln_g, cln_b,
      q_norm, w_uq_ext, kv_norm, w_ukv, inv_row, sgn_row)


def _swap_halves(w):
    half = w.shape[-1] // 2
    return jnp.concatenate([w[..., half:], w[..., :half]], axis=-1)


def kernel(x, positions, mix_norm, ffn_norm, ffn_gate, ffn_up, ffn_down, w_in_even, w_out_even,
           hgrn_lb_logits, hgrn_gnorm, sgu_ln_g, sgu_ln_b, sgu_w, sgu_b, w_in_odd, w_out_odd,
           conv_w, conv_b, conv_ln_g, conv_ln_b, mla_q_norm, mla_w_uq, mla_kv_norm, mla_w_ukv,
           final_norm):
    bsz, seq, _ = x.shape
    depth = mix_norm.shape[0]
    lower_bounds = jnp.cumsum(jax.nn.softmax(hgrn_lb_logits.astype(F32), axis=0), axis=0)

    inv = 1.0 / (ROPE_THETA ** (jnp.arange(0, MLA_ROPE, 2, dtype=F32) / MLA_ROPE))
    inv_row = jnp.tile(inv, 4)[None, :]
    sgn_row = jnp.concatenate([jnp.ones((MLA_ROPE,), F32), -jnp.ones((MLA_ROPE // 2,), F32),
                               jnp.ones((MLA_ROPE // 2,), F32)])[None, :]
    pos_bc = jnp.broadcast_to(positions.astype(F32)[:, :, None], (bsz, seq, 2 * MLA_ROPE))

    row = lambda v: v.reshape(1, -1).astype(F32)
    for layer in range(depth):
        j = layer // 2
        if layer % 2 == 0:
            sgu_b_bc = jnp.broadcast_to(sgu_b[j][:, :, None], (SGU_GROUPS, SGU_CHUNK, SGU_CH))
            x = _even_layer(x, row(mix_norm[layer]), w_in_even[j].astype(BF16),
                            w_out_even[j].astype(BF16), row(lower_bounds[j]), row(hgrn_gnorm[j]),
                            row(sgu_ln_g[j]), row(sgu_ln_b[j]), sgu_w[j], sgu_b_bc,
                            row(ffn_norm[layer]), ffn_gate[layer].astype(BF16),
                            ffn_up[layer].astype(BF16), ffn_down[layer].astype(BF16))
            continue
        else:
            w_in = w_in_odd[j]
            w_in_ext = jnp.concatenate([w_in, _swap_halves(w_in[:, ODD_ROPE:])], axis=1).astype(BF16)
            wq = mla_w_uq[j].reshape(MLA_Q_RANK, MLA_HEADS, MLA_NOPE + MLA_ROPE)
            wq_ext = jnp.concatenate([wq, _swap_halves(wq[:, :, MLA_NOPE:])], axis=2)
            wq_ext = wq_ext.reshape(MLA_Q_RANK, MLA_HEADS * Q_SLOT).astype(BF16)
            x = _odd_mixer(x, pos_bc, row(mix_norm[layer]), w_in_ext, w_out_odd[j].astype(BF16),
                           conv_w[j], row(conv_b[j]), row(conv_ln_g[j]), row(conv_ln_b[j]),
                           row(mla_q_norm[j]), wq_ext, row(mla_kv_norm[j]),
                           mla_w_ukv[j].astype(BF16), inv_row, sgn_row)
        x2 = _ffn(x.reshape(bsz * seq, D_MODEL), row(ffn_norm[layer]), ffn_gate[layer].astype(BF16),
                  ffn_up[layer].astype(BF16), ffn_down[layer].astype(BF16), row(final_norm),
                  final=(layer == depth - 1))
        x = x2.reshape(bsz, seq, D_MODEL)
    return x
```

```python
import functools
import math

import jax
import jax.numpy as jnp
from jax import lax
from jax.experimental import pallas as pl
from jax.experimental.pallas import tpu as pltpu

F32 = jnp.float32
BF16 = jnp.bfloat16

D_MODEL = 1024
MIX_HALF = D_MODEL // 2
HGRN_HEADS = 4
HGRN_DK = 128
HGRN_DV = MIX_HALF // HGRN_HEADS
HGRN_K = HGRN_HEADS * HGRN_DK
HGRN_V = HGRN_HEADS * HGRN_DV
SGU_GROUPS = 4
SGU_CH = MIX_HALF // SGU_GROUPS
SGU_CHUNK = 128
CONV_CH = MIX_HALF
CONV_WIDTH = 31
MLA_HEADS = 4
MLA_NOPE = 128
MLA_ROPE = 64
MLA_V = 128
MLA_Q_RANK = 384
MLA_KV_RANK = 256
ROPE_THETA = 10000.0
D_FF = -(-8 * D_MODEL // (3 * 256)) * 256
EPS = 1e-6
IN_EVEN = 2 * HGRN_K + 2 * HGRN_V + 2 * MIX_HALF
IN_ODD = 2 * CONV_CH + MLA_Q_RANK + MLA_KV_RANK + MLA_ROPE

LANES = 128
SUBLANES = 8
VMEM_LIMIT_BYTES = 56 * 1024 * 1024

MIX_TILE = 256
ODD_TILE = 512
FFN_TILE = 512
FFN_CHUNKS = ((0, 768), (768, 1536), (1536, 2304), (2304, D_FF))
FFN_UP_BLOCK = 512
FFN_DOWN_BLOCK = 512
HGRN_CHUNK = 64
HGRN_LEVELS = (32, 16, 8, 4)
HGRN_DIAG = 4
CONV_HALO = 32
CONV_OFF = CONV_HALO - (CONV_WIDTH - 1)
ATTN_HEAD_GROUPS = ((0, 1), (2, 3))
CONV_ROWS = 32
NEG_BIG = -1e30


def _rms(x, w):
    return x * lax.rsqrt(jnp.mean(x * x, axis=-1, keepdims=True) + EPS) * w


def _dot(a, b):
    return jnp.dot(a, b, preferred_element_type=F32)


def _dot_nt(a, b):
    return lax.dot_general(a, b, (((1,), (1,)), ((), ())), preferred_element_type=F32)


def _sigmoid(x):
    return 1.0 / (1.0 + jnp.exp(-x))


def _gelu_tanh(x):
    return 0.5 * x * (1.0 + jnp.tanh(math.sqrt(2.0 / math.pi) * (x + 0.044715 * (x * x * x))))


def _const_spec(shape):
    nd = len(shape)
    return pl.BlockSpec(shape, lambda *_: (0,) * nd, pipeline_mode=pl.Buffered(1))


class _TrailingFfn:
    def __init__(self, x_prev, nw_ref, wg_ref, wu_ref, wd_ref, act_scr, out_fn):
        self.x, self.wg_ref, self.wu_ref, self.wd_ref = x_prev, wg_ref, wu_ref, wd_ref
        self.act_scr, self.out_fn = act_scr, out_fn
        self.h = _rms(x_prev, nw_ref[...]).astype(BF16)
        up = [(fn, c0) for c0 in range(0, D_FF, FFN_UP_BLOCK) for fn in (self._gate, self._up)]
        down = [(self._down, n0) for n0 in range(0, D_MODEL, FFN_DOWN_BLOCK)]
        self.todo = up + down

    def _gate(self, c0):
        cols = slice(c0, min(c0 + FFN_UP_BLOCK, D_FF))
        g = _dot(self.h, self.wg_ref[:, cols])
        self.g_act = g * _sigmoid(g)

    def _up(self, c0):
        cols = slice(c0, min(c0 + FFN_UP_BLOCK, D_FF))
        u = _dot(self.h, self.wu_ref[:, cols])
        self.act_scr[:, cols] = (self.g_act * u).astype(BF16)

    def _down(self, n0):
        cols = slice(n0, n0 + FFN_DOWN_BLOCK)
        self.out_fn(cols, self.x[:, cols] + _dot(self.act_scr[...], self.wd_ref[:, cols]))

    def step(self, n=1):
        for _ in range(min(n, len(self.todo))):
            fn, arg = self.todo.pop(0)
            fn(arg)

    def finish(self):
        self.step(len(self.todo))


def _ffn_kernel(x_ref, nw_ref, wg_ref, wu_ref, wd_ref, fw_ref, o_ref, *, final):
    x = x_ref[...]
    h = _rms(x, nw_ref[...]).astype(BF16)
    acc = x
    for c0, c1 in FFN_CHUNKS:
        g = _dot(h, wg_ref[:, c0:c1])
        u = _dot(h, wu_ref[:, c0:c1])
        a = (g * _sigmoid(g) * u).astype(BF16)
        acc = acc + _dot(a, wd_ref[c0:c1, :])
    if final:
        acc = _rms(acc, fw_ref[...])
    o_ref[...] = acc


def _ffn(x2, norm_w, w_gate, w_up, w_down, final_w, *, final):
    n = x2.shape[0]
    tm = min(FFN_TILE, n)
    return pl.pallas_call(
        functools.partial(_ffn_kernel, final=final),
        grid=(n // tm,),
        in_specs=[
            pl.BlockSpec((tm, D_MODEL), lambda i: (i, 0)),
            _const_spec((1, D_MODEL)),
            _const_spec((D_MODEL, D_FF)),
            _const_spec((D_MODEL, D_FF)),
            _const_spec((D_FF, D_MODEL)),
            _const_spec((1, D_MODEL)),
        ],
        out_specs=pl.BlockSpec((tm, D_MODEL), lambda i: (i, 0)),
        out_shape=jax.ShapeDtypeStruct((n, D_MODEL), F32),
        compiler_params=pltpu.CompilerParams(
            dimension_semantics=("arbitrary",), vmem_limit_bytes=VMEM_LIMIT_BYTES),
        name="ffn_final" if final else "ffn",
    )(x2, norm_w, w_gate, w_up, w_down, final_w)


def _even_kernel(x_ref, nw_ref, win_ref, wout_ref, lb_ref, gn_ref, lng_ref, lnb_ref,
                 sw_ref, sb_ref, fnw_ref, wg_ref, wu_ref, wd_ref, o_ref,
                 p_scr, st_scr, kpad, bpad, msk_scr, mix_scr, xm_scr, act_scr, *, tt, nt, steps):
    s_idx = pl.program_id(0)
    t_idx = lax.rem(jnp.minimum(s_idx, steps - 1), nt)
    slot = lax.rem(s_idx, 2)
    C, PAD = HGRN_CHUNK, SUBLANES
    heads = range(HGRN_HEADS)

    @pl.when(s_idx == 0)
    def _():
        xm_scr[1] = jnp.zeros((tt, D_MODEL), F32)

    @pl.when(t_idx == 0)
    def _():
        st_scr[...] = jnp.zeros_like(st_scr)
        kpad[:, 0:PAD, :] = jnp.zeros((HGRN_HEADS, PAD, HGRN_DK), F32)
        bpad[:, 0:PAD, :] = jnp.zeros((HGRN_HEADS, PAD, HGRN_DK), F32)

    def ffn_out(cols, val):
        o_ref[0, :, cols] = val

    ffn = _TrailingFfn(xm_scr[1 - slot], fnw_ref, wg_ref, wu_ref, wd_ref, act_scr, ffn_out)
    x = x_ref[0]
    h = _rms(x, nw_ref[...]).astype(BF16)
    p_scr[...] = _dot(h, win_ref[...])

    lb = lb_ref[...]
    ti = lax.broadcasted_iota(jnp.int32, (C, C), 0)
    si = lax.broadcasted_iota(jnp.int32, (C, C), 1)
    tril_c = jnp.where(ti >= si, 1.0, 0.0).astype(BF16)
    tril3 = jnp.concatenate([tril_c, tril_c, tril_c], axis=1)
    xr = jnp.bitwise_xor(ti, si)
    for li, hs in enumerate(HGRN_LEVELS):
        own = jnp.logical_and(lax.shift_right_logical(xr, hs.bit_length() - 1) == 1, ti > si)
        msk_scr[li] = jnp.where(own, 1.0, 0.0)
    for d in range(HGRN_DIAG):
        own = jnp.logical_and(ti - si == d, xr < HGRN_DIAG)
        msk_scr[len(HGRN_LEVELS) + d] = jnp.where(own, 1.0, 0.0)

    def chunk_front(c):
        ffn.step()
        rows = slice(c * C, (c + 1) * C)
        q_all = p_scr[rows, 0:HGRN_K]
        f_pre = p_scr[rows, HGRN_K:2 * HGRN_K]
        iv_all = p_scr[rows, 2 * HGRN_K:2 * HGRN_K + HGRN_V]
        g_all = p_scr[rows, 2 * HGRN_K + HGRN_V:2 * HGRN_K + 2 * HGRN_V]
        f = lb + (1.0 - lb) * _sigmoid(f_pre)
        lf = jnp.log(f)
        lf_hi = lf.astype(BF16)
        lf_r = lf - lf_hi.astype(F32)
        lf_mid = lf_r.astype(BF16)
        lf_lo = (lf_r - lf_mid.astype(F32)).astype(BF16)
        b_all = _dot(tril3, jnp.concatenate([lf_hi, lf_mid, lf_lo], axis=0))
        k_all = 1.0 - f
        for hd in heads:
            kpad[hd, PAD:PAD + C, :] = k_all[:, hd * HGRN_DK:(hd + 1) * HGRN_DK]
            bpad[hd, PAD:PAD + C, :] = b_all[:, hd * HGRN_DK:(hd + 1) * HGRN_DK]

        def rows_of(ref, start, size):
            return jnp.concatenate([ref[hd, start:start + size, :] for hd in heads], axis=1)

        b_last = rows_of(bpad, PAD + C - 1, 1)
        q_inter = (q_all * jnp.exp(b_all)).astype(BF16)
        k_dec = (k_all * jnp.exp(b_last - b_all)).astype(BF16)
        dec_row = jnp.exp(b_last)
        iv_bf = iv_all.astype(BF16)
        gate_all = g_all * _sigmoid(g_all)

        lev = []
        for hs in HGRN_LEVELS:
            pivots = [blk * 2 * hs + hs - 1 for blk in range(C // (2 * hs))]
            piv = jnp.concatenate(
                [jnp.broadcast_to(rows_of(bpad, PAD + p, 1), (2 * hs, HGRN_K)) for p in pivots],
                axis=0)
            e = jnp.exp(-jnp.abs(b_all - piv))
            lev.append(((q_all * e).astype(BF16), (k_all * e).astype(BF16)))
        diag = [q_all * k_all]
        for d in range(1, HGRN_DIAG):
            k_sh = rows_of(kpad, PAD - d, C)
            b_sh = rows_of(bpad, PAD - d, C)
            diag.append(q_all * k_sh * jnp.exp(b_all - b_sh))
        return rows, lev, diag, q_inter, k_dec, dec_row, iv_all, iv_bf, gate_all

    def chunk_back(front):
        rows, lev, diag, q_inter, k_dec, dec_row, iv_all, iv_bf, gate_all = front
        for hd in heads:
            if hd != 1:
                ffn.step()
            sl = slice(hd * HGRN_DK, (hd + 1) * HGRN_DK)
            vsl = slice(hd * HGRN_DV, (hd + 1) * HGRN_DV)
            sc = jnp.zeros((C, C), F32)
            for li in range(len(HGRN_LEVELS)):
                sc = sc + _dot_nt(lev[li][0][:, sl], lev[li][1][:, sl]) * msk_scr[li]
            for d in range(HGRN_DIAG):
                sc = sc + (jnp.sum(diag[d][:, sl], axis=-1, keepdims=True)
                           * msk_scr[len(HGRN_LEVELS) + d])
            st = st_scr[hd]
            o = _dot_nt(q_inter[:, sl], st.astype(BF16)) + _dot(sc.astype(BF16), iv_bf[:, vsl])
            st_scr[hd] = st * dec_row[:, sl] + _dot(iv_all[:, vsl].T.astype(BF16), k_dec[:, sl])
            on = _rms(o, gn_ref[:, vsl])
            mix_scr[rows, vsl] = (on * gate_all[:, vsl]).astype(BF16)

    front = chunk_front(0)
    for c in range(tt // C):
        nxt = chunk_front(c + 1) if c + 1 < tt // C else None
        chunk_back(front)
        front = nxt

    prow = lax.broadcasted_iota(jnp.int32, (SGU_CHUNK, SGU_CHUNK), 0)
    pcol = lax.broadcasted_iota(jnp.int32, (SGU_CHUNK, SGU_CHUNK), 1)
    w_causal = [jnp.where(prow >= pcol, sw_ref[gi], 0.0).astype(BF16) for gi in range(SGU_GROUPS)]
    for n in range(tt // SGU_CHUNK):
        rows = slice(n * SGU_CHUNK, (n + 1) * SGU_CHUNK)
        for gi in range(SGU_GROUPS):
            if gi % 2 == 0:
                ffn.step()
            csl = slice(gi * SGU_CH, (gi + 1) * SGU_CH)
            u = _gelu_tanh(p_scr[rows, 2 * HGRN_K + 2 * HGRN_V + gi * SGU_CH:
                                 2 * HGRN_K + 2 * HGRN_V + (gi + 1) * SGU_CH])
            v = _gelu_tanh(p_scr[rows, 2 * HGRN_K + 2 * HGRN_V + MIX_HALF + gi * SGU_CH:
                                 2 * HGRN_K + 2 * HGRN_V + MIX_HALF + (gi + 1) * SGU_CH])
            mu = jnp.mean(v, axis=-1, keepdims=True)
            vc = v - mu
            var = jnp.mean(vc * vc, axis=-1, keepdims=True)
            vn = vc * lax.rsqrt(var + EPS) * lng_ref[:, csl] + lnb_ref[:, csl]
            z = _dot(w_causal[gi], vn.astype(BF16)) + sb_ref[gi]
            mix_scr[rows, HGRN_V + gi * SGU_CH:HGRN_V + (gi + 1) * SGU_CH] = (u * z).astype(BF16)

    xm_scr[slot] = x + _dot(mix_scr[...], wout_ref[...])
    ffn.finish()


def _tile_index_maps(nt, steps):
    def mixer_tile(s):
        m = jnp.minimum(s, steps - 1)
        return m // nt, m % nt, 0

    def ffn_tile(s):
        f = jnp.maximum(s - 1, 0)
        return f // nt, f % nt, 0

    return mixer_tile, ffn_tile


def _even_layer(x, norm_w, w_in, w_out, lb, gnorm, ln_g, ln_b, sgu_w, sgu_b_bc,
                ffn_norm_w, w_gate, w_up, w_down):
    bsz, seq, _ = x.shape
    tt = min(MIX_TILE, seq)
    nt = seq // tt
    steps = bsz * nt
    mixer_tile, ffn_tile = _tile_index_maps(nt, steps)
    return pl.pallas_call(
        functools.partial(_even_kernel, tt=tt, nt=nt, steps=steps),
        grid=(steps + 1,),
        in_specs=[
            pl.BlockSpec((1, tt, D_MODEL), mixer_tile),
            _const_spec((1, D_MODEL)),
            _const_spec((D_MODEL, IN_EVEN)),
            _const_spec((D_MODEL, D_MODEL)),
            _const_spec((1, HGRN_K)),
            _const_spec((1, HGRN_V)),
            _const_spec((1, MIX_HALF)),
            _const_spec((1, MIX_HALF)),
            _const_spec((SGU_GROUPS, SGU_CHUNK, SGU_CHUNK)),
            _const_spec((SGU_GROUPS, SGU_CHUNK, SGU_CH)),
            _const_spec((1, D_MODEL)),
            _const_spec((D_MODEL, D_FF)),
            _const_spec((D_MODEL, D_FF)),
            _const_spec((D_FF, D_MODEL)),
        ],
        out_specs=pl.BlockSpec((1, tt, D_MODEL), ffn_tile),
        out_shape=jax.ShapeDtypeStruct(x.shape, F32),
        scratch_shapes=[
            pltpu.VMEM((tt, IN_EVEN), F32),
            pltpu.VMEM((HGRN_HEADS, HGRN_DV, HGRN_DK), F32),
            pltpu.VMEM((HGRN_HEADS, SUBLANES + HGRN_CHUNK, HGRN_DK), F32),
            pltpu.VMEM((HGRN_HEADS, SUBLANES + HGRN_CHUNK, HGRN_DK), F32),
            pltpu.VMEM((len(HGRN_LEVELS) + HGRN_DIAG, HGRN_CHUNK, HGRN_CHUNK), F32),
            pltpu.VMEM((tt, D_MODEL), BF16),
            pltpu.VMEM((2, tt, D_MODEL), F32),
            pltpu.VMEM((tt, D_FF), BF16),
        ],
        compiler_params=pltpu.CompilerParams(
            dimension_semantics=("arbitrary",), vmem_limit_bytes=VMEM_LIMIT_BYTES),
        name="even_layer",
    )(x, norm_w, w_in, w_out, lb, gnorm, ln_g, ln_b, sgu_w, sgu_b_bc,
      ffn_norm_w, w_gate, w_up, w_down)


ODD_CQ = 2 * CONV_CH
ODD_CKV = ODD_CQ + MLA_Q_RANK
ODD_ROPE = ODD_CKV + MLA_KV_RANK
IN_ODD_EXT = ODD_ROPE + 2 * MLA_ROPE
Q_SLOT = MLA_NOPE + 2 * MLA_ROPE


def _odd_kernel(x_ref, pos_ref, nw_ref, win_ref, wout_ref, cw_ref, cb_ref, clg_ref, clb_ref,
                qn_ref, wuq_ref, kvn_ref, wukv_ref, inv_ref, sgn_ref, o_ref,
                p_scr, hpad, cacc_scr, k_scr, vt_scr, q_scr, m_scr, l_scr, acc_scr, s_scr, mix_scr,
                *, tt):
    t_idx = pl.program_id(1)
    scale = (MLA_NOPE + MLA_ROPE) ** -0.5

    @pl.when(t_idx == 0)
    def _():
        hpad[:, 0:CONV_HALO, :] = jnp.zeros((CONV_CH // LANES, CONV_HALO, LANES), F32)

    x = x_ref[0]
    h = _rms(x, nw_ref[...]).astype(BF16)
    p_scr[...] = _dot(h, win_ref[...])

    for cb in range(CONV_CH // LANES):
        cs = slice(cb * LANES, (cb + 1) * LANES)
        a = p_scr[:, cs]
        gate = p_scr[:, CONV_CH + cb * LANES:CONV_CH + (cb + 1) * LANES]
        hpad[cb, CONV_HALO:CONV_HALO + tt, :] = a * _sigmoid(gate)
    for r in range(0, tt, CONV_ROWS):
        for cb in range(CONV_CH // LANES):
            cs = slice(cb * LANES, (cb + 1) * LANES)
            acc = jnp.broadcast_to(cb_ref[:, cs], (CONV_ROWS, LANES))
            for w in range(CONV_WIDTH):
                r0w = r + CONV_OFF + w
                acc = acc + hpad[cb, r0w:r0w + CONV_ROWS, :] * cw_ref[w:w + 1, cs]
            cacc_scr[r:r + CONV_ROWS, cs] = acc
    for cb in range(CONV_CH // LANES):
        hpad[cb, 0:CONV_HALO, :] = hpad[cb, tt:tt + CONV_HALO, :]
    acc = cacc_scr[...]
    mu = jnp.mean(acc, axis=-1, keepdims=True)
    ac = acc - mu
    var = jnp.mean(ac * ac, axis=-1, keepdims=True)
    cn = ac * lax.rsqrt(var + EPS) * clg_ref[...] + clb_ref[...]
    mix_scr[:, 0:CONV_CH] = (cn * _sigmoid(cn)).astype(BF16)

    cq = _rms(p_scr[:, ODD_CQ:ODD_CKV], qn_ref[...]).astype(BF16)
    ckv = _rms(p_scr[:, ODD_CKV:ODD_ROPE], kvn_ref[...]).astype(BF16)
    qf = _dot(cq, wuq_ref[...])
    kvf = _dot(ckv, wukv_ref[...])

    ang = pos_ref[0] * inv_ref[...]
    cos = jnp.cos(ang)
    sin_s = jnp.sin(ang) * sgn_ref[...]
    rot = jnp.concatenate([cos[:, :MLA_ROPE], sin_s[:, MLA_ROPE:]], axis=1)
    kr = p_scr[:, ODD_ROPE:IN_ODD_EXT] * rot
    kr = kr + pltpu.roll(kr, MLA_ROPE, axis=1)
    q_mult = jnp.concatenate([jnp.full((tt, MLA_NOPE), scale, F32), rot * scale], axis=1)

    r0 = pl.multiple_of(t_idx * tt, tt)
    for hd in range(MLA_HEADS):
        q_scr[hd] = (qf[:, hd * Q_SLOT:(hd + 1) * Q_SLOT] * q_mult).astype(BF16)
        kv0 = hd * (MLA_NOPE + MLA_V)
        k_scr[hd, pl.ds(r0, tt), 0:MLA_NOPE] = kvf[:, kv0:kv0 + MLA_NOPE].astype(BF16)
        k_scr[hd, pl.ds(r0, tt), MLA_NOPE:Q_SLOT] = kr.astype(BF16)
        vt_scr[hd, t_idx] = kvf[:, kv0 + MLA_NOPE:kv0 + MLA_NOPE + MLA_V].T.astype(BF16)
        m_scr[hd] = jnp.full((1, tt), NEG_BIG, F32)
        l_scr[hd] = jnp.zeros((1, tt), F32)
        acc_scr[hd] = jnp.zeros((MLA_V, tt), F32)

    heads = range(MLA_HEADS)

    def scores_into(kb, slot, group):
        k0 = pl.multiple_of(kb * tt, tt)
        for hd in group:
            s_scr[slot, hd] = _dot_nt(k_scr[hd, pl.ds(k0, tt), :], q_scr[hd])

    def attend(kb, masked):
        for group in ATTN_HEAD_GROUPS:
            sts = {hd: s_scr[kb % 2, hd] for hd in group}
            if not masked:
                scores_into(kb + 1, (kb + 1) % 2, group)
            if masked:
                krow = lax.broadcasted_iota(jnp.int32, (tt, tt), 0)
                qcol = lax.broadcasted_iota(jnp.int32, (tt, tt), 1)
                sts = {hd: jnp.where(krow <= qcol, sts[hd], NEG_BIG) for hd in group}
            m_olds = {hd: m_scr[hd] for hd in group}
            m_news = {hd: jnp.maximum(m_olds[hd], jnp.max(sts[hd], axis=0, keepdims=True))
                      for hd in group}
            alphas = {hd: jnp.exp(m_olds[hd] - m_news[hd]) for hd in group}
            prs = {hd: jnp.exp(sts[hd] - m_news[hd]) for hd in group}
            for hd in group:
                l_scr[hd] = alphas[hd] * l_scr[hd] + jnp.sum(prs[hd], axis=0, keepdims=True)
                m_scr[hd] = m_news[hd]
            pvs = {hd: _dot(vt_scr[hd, kb], prs[hd].astype(BF16)) for hd in group}
            for hd in group:
                acc_scr[hd] = acc_scr[hd] * alphas[hd] + pvs[hd]

    def kv_body(kb, carry):
        attend(kb, False)
        return carry

    scores_into(0, 0, heads)
    lax.fori_loop(0, t_idx, kv_body, 0)
    attend(t_idx, True)
    for hd in range(MLA_HEADS):
        o_t = acc_scr[hd] * (1.0 / l_scr[hd])
        mix_scr[:, CONV_CH + hd * MLA_V:CONV_CH + (hd + 1) * MLA_V] = o_t.T.astype(BF16)

    o_ref[0] = x + _dot(mix_scr[...], wout_ref[...])


def _odd_mixer(x, pos_bc, norm_w, w_in_ext, w_out, conv_w, conv_b, cln_g, cln_b,
               q_norm, w_uq_ext, kv_norm, w_ukv, inv_row, sgn_row):
    bsz, seq, _ = x.shape
    tt = min(ODD_TILE, seq)
    return pl.pallas_call(
        functools.partial(_odd_kernel, tt=tt),
        grid=(bsz, seq // tt),
        in_specs=[
            pl.BlockSpec((1, tt, D_MODEL), lambda b, t: (b, t, 0)),
            pl.BlockSpec((1, tt, 2 * MLA_ROPE), lambda b, t: (b, t, 0)),
            _const_spec((1, D_MODEL)),
            _const_spec((D_MODEL, IN_ODD_EXT)),
            _const_spec((D_MODEL, D_MODEL)),
            _const_spec((CONV_WIDTH, CONV_CH)),
            _const_spec((1, CONV_CH)),
            _const_spec((1, CONV_CH)),
            _const_spec((1, CONV_CH)),
            _const_spec((1, MLA_Q_RANK)),
            _const_spec((MLA_Q_RANK, MLA_HEADS * Q_SLOT)),
            _const_spec((1, MLA_KV_RANK)),
            _const_spec((MLA_KV_RANK, MLA_HEADS * (MLA_NOPE + MLA_V))),
            _const_spec((1, 2 * MLA_ROPE)),
            _const_spec((1, 2 * MLA_ROPE)),
        ],
        out_specs=pl.BlockSpec((1, tt, D_MODEL), lambda b, t: (b, t, 0)),
        out_shape=jax.ShapeDtypeStruct(x.shape, F32),
        scratch_shapes=[
            pltpu.VMEM((tt, IN_ODD_EXT), F32),
            pltpu.VMEM((CONV_CH // LANES, CONV_HALO + tt, LANES), F32),
            pltpu.VMEM((tt, CONV_CH), F32),
            pltpu.VMEM((MLA_HEADS, seq, Q_SLOT), BF16),
            pltpu.VMEM((MLA_HEADS, seq // tt, MLA_V, tt), BF16),
            pltpu.VMEM((MLA_HEADS, tt, Q_SLOT), BF16),
            pltpu.VMEM((MLA_HEADS, 1, tt), F32),
            pltpu.VMEM((MLA_HEADS, 1, tt), F32),
            pltpu.VMEM((MLA_HEADS, MLA_V, tt), F32),
            pltpu.VMEM((2, MLA_HEADS, tt, tt), F32),
            pltpu.VMEM((tt, D_MODEL), BF16),
        ],
        compiler_params=pltpu.CompilerParams(
            dimension_semantics=("arbitrary", "arbitrary"), vmem_limit_bytes=VMEM_LIMIT_BYTES),
        name="odd_mixer",
    )(x, pos_bc, norm_w, w_in_ext, w_out, conv_w, conv_b, cln_g, cln_b,
      q_norm, w_uq_ext, kv_norm, w_ukv, inv_row, sgn_row)


def _swap_halves(w):
    half = w.shape[-1] // 2
    return jnp.concatenate([w[..., half:], w[..., :half]], axis=-1)


def kernel(x, positions, mix_norm, ffn_norm, ffn_gate, ffn_up, ffn_down, w_in_even, w_out_even,
           hgrn_lb_logits, hgrn_gnorm, sgu_ln_g, sgu_ln_b, sgu_w, sgu_b, w_in_odd, w_out_odd,
           conv_w, conv_b, conv_ln_g, conv_ln_b, mla_q_norm, mla_w_uq, mla_kv_norm, mla_w_ukv,
           final_norm):
    bsz, seq, _ = x.shape
    depth = mix_norm.shape[0]
    lower_bounds = jnp.cumsum(jax.nn.softmax(hgrn_lb_logits.astype(F32), axis=0), axis=0)

    inv = 1.0 / (ROPE_THETA ** (jnp.arange(0, MLA_ROPE, 2, dtype=F32) / MLA_ROPE))
    inv_row = jnp.tile(inv, 4)[None, :]
    sgn_row = jnp.concatenate([jnp.ones((MLA_ROPE,), F32), -jnp.ones((MLA_ROPE // 2,), F32),
                               jnp.ones((MLA_ROPE // 2,), F32)])[None, :]
    pos_bc = jnp.broadcast_to(positions.astype(F32)[:, :, None], (bsz, seq, 2 * MLA_ROPE))

    row = lambda v: v.reshape(1, -1).astype(F32)
    for layer in range(depth):
        j = layer // 2
        if layer % 2 == 0:
            sgu_b_bc = jnp.broadcast_to(sgu_b[j][:, :, None], (SGU_GROUPS, SGU_CHUNK, SGU_CH))
            x = _even_layer(x, row(mix_norm[layer]), w_in_even[j].astype(BF16),
                            w_out_even[j].astype(BF16), row(lower_bounds[j]), row(hgrn_gnorm[j]),
                            row(sgu_ln_g[j]), row(sgu_ln_b[j]), sgu_w[j], sgu_b_bc,
                            row(ffn_norm[layer]), ffn_gate[layer].astype(BF16),
                            ffn_up[layer].astype(BF16), ffn_down[layer].astype(BF16))
            continue
        else:
            w_in = w_in_odd[j]
            w_in_ext = jnp.concatenate([w_in, _swap_halves(w_in[:, ODD_ROPE:])], axis=1).astype(BF16)
            wq = mla_w_uq[j].reshape(MLA_Q_RANK, MLA_HEADS, MLA_NOPE + MLA_ROPE)
            wq_ext = jnp.concatenate([wq, _swap_halves(wq[:, :, MLA_NOPE:])], axis=2)
            wq_ext = wq_ext.reshape(MLA_Q_RANK, MLA_HEADS * Q_SLOT).astype(BF16)
            x = _odd_mixer(x, pos_bc, row(mix_norm[layer]), w_in_ext, w_out_odd[j].astype(BF16),
                           conv_w[j], row(conv_b[j]), row(conv_ln_g[j]), row(conv_ln_b[j]),
                           row(mla_q_norm[j]), wq_ext, row(mla_kv_norm[j]),
                           mla_w_ukv[j].astype(BF16), inv_row, sgn_row)
        x2 = _ffn(x.reshape(bsz * seq, D_MODEL), row(ffn_norm[layer]), ffn_gate[layer].astype(BF16),
                  ffn_up[layer].astype(BF16), ffn_down[layer].astype(BF16), row(final_norm),
                  final=(layer == depth - 1))
        x = x2.reshape(bsz, seq, D_MODEL)
    return x
```

```python
import functools
import math

import jax
import jax.numpy as jnp
from jax import lax
from jax.experimental import pallas as pl
from jax.experimental.pallas import tpu as pltpu

F32 = jnp.float32
BF16 = jnp.bfloat16

D_MODEL = 1024
MIX_HALF = D_MODEL // 2
HGRN_HEADS = 4
HGRN_DK = 128
HGRN_DV = MIX_HALF // HGRN_HEADS
HGRN_K = HGRN_HEADS * HGRN_DK
HGRN_V = HGRN_HEADS * HGRN_DV
SGU_GROUPS = 4
SGU_CH = MIX_HALF // SGU_GROUPS
SGU_CHUNK = 128
CONV_CH = MIX_HALF
CONV_WIDTH = 31
MLA_HEADS = 4
MLA_NOPE = 128
MLA_ROPE = 64
MLA_V = 128
MLA_Q_RANK = 384
MLA_KV_RANK = 256
ROPE_THETA = 10000.0
D_FF = -(-8 * D_MODEL // (3 * 256)) * 256
EPS = 1e-6
IN_EVEN = 2 * HGRN_K + 2 * HGRN_V + 2 * MIX_HALF
IN_ODD = 2 * CONV_CH + MLA_Q_RANK + MLA_KV_RANK + MLA_ROPE

LANES = 128
SUBLANES = 8
VMEM_LIMIT_BYTES = 56 * 1024 * 1024

MIX_TILE = 512
ODD_TILE = 512
FFN_TILE = 1024
FFN_CHUNKS = ((0, 768), (768, 1536), (1536, 2304), (2304, D_FF))
FFN_UP_BLOCK = 512
FFN_DOWN_BLOCK = 512
HGRN_CHUNK = 64
HGRN_LEVELS = (32, 16, 8, 4)
HGRN_DIAG = 4
CONV_HALO = 32
CONV_OFF = CONV_HALO - (CONV_WIDTH - 1)
CONV_ROWS = 32
NEG_BIG = -1e30


def _rms(x, w):
    return x * lax.rsqrt(jnp.mean(x * x, axis=-1, keepdims=True) + EPS) * w


def _dot(a, b):
    return jnp.dot(a, b, preferred_element_type=F32)


def _dot_nt(a, b):
    return lax.dot_general(a, b, (((1,), (1,)), ((), ())), preferred_element_type=F32)


def _sigmoid(x):
    return 1.0 / (1.0 + jnp.exp(-x))


def _gelu_tanh(x):
    return 0.5 * x * (1.0 + jnp.tanh(math.sqrt(2.0 / math.pi) * (x + 0.044715 * (x * x * x))))


def _const_spec(shape):
    nd = len(shape)
    return pl.BlockSpec(shape, lambda *_: (0,) * nd, pipeline_mode=pl.Buffered(1))


class _TrailingFfn:
    def __init__(self, x_prev, nw_ref, wg_ref, wu_ref, wd_ref, act_scr, out_fn):
        self.x, self.wg_ref, self.wu_ref, self.wd_ref = x_prev, wg_ref, wu_ref, wd_ref
        self.act_scr, self.out_fn = act_scr, out_fn
        self.h = _rms(x_prev, nw_ref[...]).astype(BF16)
        up = [(self._up, c0) for c0 in range(0, D_FF, FFN_UP_BLOCK)]
        down = [(self._down, n0) for n0 in range(0, D_MODEL, FFN_DOWN_BLOCK)]
        self.todo = up + down

    def _up(self, c0):
        cols = slice(c0, min(c0 + FFN_UP_BLOCK, D_FF))
        g = _dot(self.h, self.wg_ref[:, cols])
        u = _dot(self.h, self.wu_ref[:, cols])
        self.act_scr[:, cols] = (g * _sigmoid(g) * u).astype(BF16)

    def _down(self, n0):
        cols = slice(n0, n0 + FFN_DOWN_BLOCK)
        self.out_fn(cols, self.x[:, cols] + _dot(self.act_scr[...], self.wd_ref[:, cols]))

    def step(self, n=1):
        for _ in range(min(n, len(self.todo))):
            fn, arg = self.todo.pop(0)
            fn(arg)

    def finish(self):
        self.step(len(self.todo))


def _ffn_kernel(x_ref, nw_ref, wg_ref, wu_ref, wd_ref, fw_ref, o_ref, *, final):
    x = x_ref[...]
    h = _rms(x, nw_ref[...]).astype(BF16)
    acc = x
    for c0, c1 in FFN_CHUNKS:
        g = _dot(h, wg_ref[:, c0:c1])
        u = _dot(h, wu_ref[:, c0:c1])
        a = (g * _sigmoid(g) * u).astype(BF16)
        acc = acc + _dot(a, wd_ref[c0:c1, :])
    if final:
        acc = _rms(acc, fw_ref[...])
    o_ref[...] = acc


def _ffn(x2, norm_w, w_gate, w_up, w_down, final_w, *, final):
    n = x2.shape[0]
    tm = min(FFN_TILE, n)
    return pl.pallas_call(
        functools.partial(_ffn_kernel, final=final),
        grid=(n // tm,),
        in_specs=[
            pl.BlockSpec((tm, D_MODEL), lambda i: (i, 0)),
            _const_spec((1, D_MODEL)),
            _const_spec((D_MODEL, D_FF)),
            _const_spec((D_MODEL, D_FF)),
            _const_spec((D_FF, D_MODEL)),
            _const_spec((1, D_MODEL)),
        ],
        out_specs=pl.BlockSpec((tm, D_MODEL), lambda i: (i, 0)),
        out_shape=jax.ShapeDtypeStruct((n, D_MODEL), F32),
        compiler_params=pltpu.CompilerParams(
            dimension_semantics=("arbitrary",), vmem_limit_bytes=VMEM_LIMIT_BYTES),
        name="ffn_final" if final else "ffn",
    )(x2, norm_w, w_gate, w_up, w_down, final_w)


def _even_kernel(x_ref, nw_ref, win_ref, wout_ref, lb_ref, gn_ref, lng_ref, lnb_ref,
                 sw_ref, sb_ref, fnw_ref, wg_ref, wu_ref, wd_ref, o_ref,
                 p_scr, st_scr, kpad, bpad, msk_scr, mix_scr, xm_scr, act_scr, *, tt, nt, steps):
    s_idx = pl.program_id(0)
    t_idx = lax.rem(jnp.minimum(s_idx, steps - 1), nt)
    slot = lax.rem(s_idx, 2)
    C, PAD = HGRN_CHUNK, SUBLANES
    heads = range(HGRN_HEADS)

    @pl.when(s_idx == 0)
    def _():
        xm_scr[1] = jnp.zeros((tt, D_MODEL), F32)

    @pl.when(t_idx == 0)
    def _():
        st_scr[...] = jnp.zeros_like(st_scr)
        kpad[:, 0:PAD, :] = jnp.zeros((HGRN_HEADS, PAD, HGRN_DK), F32)
        bpad[:, 0:PAD, :] = jnp.zeros((HGRN_HEADS, PAD, HGRN_DK), F32)

    def ffn_out(cols, val):
        o_ref[0, :, cols] = val

    ffn = _TrailingFfn(xm_scr[1 - slot], fnw_ref, wg_ref, wu_ref, wd_ref, act_scr, ffn_out)
    x = x_ref[0]
    h = _rms(x, nw_ref[...]).astype(BF16)
    p_scr[...] = _dot(h, win_ref[...])

    lb = lb_ref[...]
    ti = lax.broadcasted_iota(jnp.int32, (C, C), 0)
    si = lax.broadcasted_iota(jnp.int32, (C, C), 1)
    tril_c = jnp.where(ti >= si, 1.0, 0.0).astype(BF16)
    tril3 = jnp.concatenate([tril_c, tril_c, tril_c], axis=1)
    xr = jnp.bitwise_xor(ti, si)
    for li, hs in enumerate(HGRN_LEVELS):
        own = jnp.logical_and(lax.shift_right_logical(xr, hs.bit_length() - 1) == 1, ti > si)
        msk_scr[li] = jnp.where(own, 1.0, 0.0)
    for d in range(HGRN_DIAG):
        own = jnp.logical_and(ti - si == d, xr < HGRN_DIAG)
        msk_scr[len(HGRN_LEVELS) + d] = jnp.where(own, 1.0, 0.0)

    for c in range(tt // C):
        ffn.step()
        rows = slice(c * C, (c + 1) * C)
        q_all = p_scr[rows, 0:HGRN_K]
        f_pre = p_scr[rows, HGRN_K:2 * HGRN_K]
        iv_all = p_scr[rows, 2 * HGRN_K:2 * HGRN_K + HGRN_V]
        g_all = p_scr[rows, 2 * HGRN_K + HGRN_V:2 * HGRN_K + 2 * HGRN_V]
        f = lb + (1.0 - lb) * _sigmoid(f_pre)
        lf = jnp.log(f)
        lf_hi = lf.astype(BF16)
        lf_r = lf - lf_hi.astype(F32)
        lf_mid = lf_r.astype(BF16)
        lf_lo = (lf_r - lf_mid.astype(F32)).astype(BF16)
        b_all = _dot(tril3, jnp.concatenate([lf_hi, lf_mid, lf_lo], axis=0))
        k_all = 1.0 - f
        for hd in heads:
            kpad[hd, PAD:PAD + C, :] = k_all[:, hd * HGRN_DK:(hd + 1) * HGRN_DK]
            bpad[hd, PAD:PAD + C, :] = b_all[:, hd * HGRN_DK:(hd + 1) * HGRN_DK]

        def rows_of(ref, start, size):
            return jnp.concatenate([ref[hd, start:start + size, :] for hd in heads], axis=1)

        b_last = rows_of(bpad, PAD + C - 1, 1)
        q_inter = (q_all * jnp.exp(b_all)).astype(BF16)
        k_dec = (k_all * jnp.exp(b_last - b_all)).astype(BF16)
        dec_row = jnp.exp(b_last)
        iv_bf = iv_all.astype(BF16)
        gate_all = g_all * _sigmoid(g_all)

        lev = []
        for hs in HGRN_LEVELS:
            pivots = [blk * 2 * hs + hs - 1 for blk in range(C // (2 * hs))]
            piv = jnp.concatenate(
                [jnp.broadcast_to(rows_of(bpad, PAD + p, 1), (2 * hs, HGRN_K)) for p in pivots],
                axis=0)
            e = jnp.exp(-jnp.abs(b_all - piv))
            lev.append(((q_all * e).astype(BF16), (k_all * e).astype(BF16)))
        diag = [q_all * k_all]
        for d in range(1, HGRN_DIAG):
            k_sh = rows_of(kpad, PAD - d, C)
            b_sh = rows_of(bpad, PAD - d, C)
            diag.append(q_all * k_sh * jnp.exp(b_all - b_sh))

        for hd in heads:
            if hd % 2 == 0:
                ffn.step()
            sl = slice(hd * HGRN_DK, (hd + 1) * HGRN_DK)
            vsl = slice(hd * HGRN_DV, (hd + 1) * HGRN_DV)
            sc = jnp.zeros((C, C), F32)
            for li in range(len(HGRN_LEVELS)):
                sc = sc + _dot_nt(lev[li][0][:, sl], lev[li][1][:, sl]) * msk_scr[li]
            for d in range(HGRN_DIAG):
                sc = sc + (jnp.sum(diag[d][:, sl], axis=-1, keepdims=True)
                           * msk_scr[len(HGRN_LEVELS) + d])
            st = st_scr[hd]
            o = _dot_nt(q_inter[:, sl], st.astype(BF16)) + _dot(sc.astype(BF16), iv_bf[:, vsl])
            st_scr[hd] = st * dec_row[:, sl] + _dot(iv_all[:, vsl].T.astype(BF16), k_dec[:, sl])
            on = _rms(o, gn_ref[:, vsl])
            mix_scr[rows, vsl] = (on * gate_all[:, vsl]).astype(BF16)

    prow = lax.broadcasted_iota(jnp.int32, (SGU_CHUNK, SGU_CHUNK), 0)
    pcol = lax.broadcasted_iota(jnp.int32, (SGU_CHUNK, SGU_CHUNK), 1)
    w_causal = [jnp.where(prow >= pcol, sw_ref[gi], 0.0).astype(BF16) for gi in range(SGU_GROUPS)]
    for n in range(tt // SGU_CHUNK):
        rows = slice(n * SGU_CHUNK, (n + 1) * SGU_CHUNK)
        for gi in range(SGU_GROUPS):
            if gi % 2 == 0:
                ffn.step()
            csl = slice(gi * SGU_CH, (gi + 1) * SGU_CH)
            u = _gelu_tanh(p_scr[rows, 2 * HGRN_K + 2 * HGRN_V + gi * SGU_CH:
                                 2 * HGRN_K + 2 * HGRN_V + (gi + 1) * SGU_CH])
            v = _gelu_tanh(p_scr[rows, 2 * HGRN_K + 2 * HGRN_V + MIX_HALF + gi * SGU_CH:
                                 2 * HGRN_K + 2 * HGRN_V + MIX_HALF + (gi + 1) * SGU_CH])
            mu = jnp.mean(v, axis=-1, keepdims=True)
            vc = v - mu
            var = jnp.mean(vc * vc, axis=-1, keepdims=True)
            vn = vc * lax.rsqrt(var + EPS) * lng_ref[:, csl] + lnb_ref[:, csl]
            z = _dot(w_causal[gi], vn.astype(BF16)) + sb_ref[gi]
            mix_scr[rows, HGRN_V + gi * SGU_CH:HGRN_V + (gi + 1) * SGU_CH] = (u * z).astype(BF16)

    xm_scr[slot] = x + _dot(mix_scr[...], wout_ref[...])
    ffn.finish()


def _tile_index_maps(nt, steps):
    def mixer_tile(s):
        m = jnp.minimum(s, steps - 1)
        return m // nt, m % nt, 0

    def ffn_tile(s):
        f = jnp.maximum(s - 1, 0)
        return f // nt, f % nt, 0

    return mixer_tile, ffn_tile


def _even_layer(x, norm_w, w_in, w_out, lb, gnorm, ln_g, ln_b, sgu_w, sgu_b_bc,
                ffn_norm_w, w_gate, w_up, w_down):
    bsz, seq, _ = x.shape
    tt = min(MIX_TILE, seq)
    nt = seq // tt
    steps = bsz * nt
    mixer_tile, ffn_tile = _tile_index_maps(nt, steps)
    return pl.pallas_call(
        functools.partial(_even_kernel, tt=tt, nt=nt, steps=steps),
        grid=(steps + 1,),
        in_specs=[
            pl.BlockSpec((1, tt, D_MODEL), mixer_tile),
            _const_spec((1, D_MODEL)),
            _const_spec((D_MODEL, IN_EVEN)),
            _const_spec((D_MODEL, D_MODEL)),
            _const_spec((1, HGRN_K)),
            _const_spec((1, HGRN_V)),
            _const_spec((1, MIX_HALF)),
            _const_spec((1, MIX_HALF)),
            _const_spec((SGU_GROUPS, SGU_CHUNK, SGU_CHUNK)),
            _const_spec((SGU_GROUPS, SGU_CHUNK, SGU_CH)),
            _const_spec((1, D_MODEL)),
            _const_spec((D_MODEL, D_FF)),
            _const_spec((D_MODEL, D_FF)),
            _const_spec((D_FF, D_MODEL)),
        ],
        out_specs=pl.BlockSpec((1, tt, D_MODEL), ffn_tile),
        out_shape=jax.ShapeDtypeStruct(x.shape, F32),
        scratch_shapes=[
            pltpu.VMEM((tt, IN_EVEN), F32),
            pltpu.VMEM((HGRN_HEADS, HGRN_DV, HGRN_DK), F32),
            pltpu.VMEM((HGRN_HEADS, SUBLANES + HGRN_CHUNK, HGRN_DK), F32),
            pltpu.VMEM((HGRN_HEADS, SUBLANES + HGRN_CHUNK, HGRN_DK), F32),
            pltpu.VMEM((len(HGRN_LEVELS) + HGRN_DIAG, HGRN_CHUNK, HGRN_CHUNK), F32),
            pltpu.VMEM((tt, D_MODEL), BF16),
            pltpu.VMEM((2, tt, D_MODEL), F32),
            pltpu.VMEM((tt, D_FF), BF16),
        ],
        compiler_params=pltpu.CompilerParams(
            dimension_semantics=("arbitrary",), vmem_limit_bytes=VMEM_LIMIT_BYTES),
        name="even_layer",
    )(x, norm_w, w_in, w_out, lb, gnorm, ln_g, ln_b, sgu_w, sgu_b_bc,
      ffn_norm_w, w_gate, w_up, w_down)


ODD_CQ = 2 * CONV_CH
ODD_CKV = ODD_CQ + MLA_Q_RANK
ODD_ROPE = ODD_CKV + MLA_KV_RANK
IN_ODD_EXT = ODD_ROPE + 2 * MLA_ROPE
Q_SLOT = MLA_NOPE + 2 * MLA_ROPE


def _odd_kernel(x_ref, pos_ref, nw_ref, win_ref, wout_ref, cw_ref, cb_ref, clg_ref, clb_ref,
                qn_ref, wuq_ref, kvn_ref, wukv_ref, inv_ref, sgn_ref, o_ref,
                p_scr, hpad, cacc_scr, k_scr, vt_scr, q_scr, m_scr, l_scr, acc_scr, s_scr, mix_scr,
                *, tt):
    t_idx = pl.program_id(1)
    scale = (MLA_NOPE + MLA_ROPE) ** -0.5

    @pl.when(t_idx == 0)
    def _():
        hpad[:, 0:CONV_HALO, :] = jnp.zeros((CONV_CH // LANES, CONV_HALO, LANES), F32)

    x = x_ref[0]
    h = _rms(x, nw_ref[...]).astype(BF16)
    p_scr[...] = _dot(h, win_ref[...])

    for cb in range(CONV_CH // LANES):
        cs = slice(cb * LANES, (cb + 1) * LANES)
        a = p_scr[:, cs]
        gate = p_scr[:, CONV_CH + cb * LANES:CONV_CH + (cb + 1) * LANES]
        hpad[cb, CONV_HALO:CONV_HALO + tt, :] = a * _sigmoid(gate)
    for r in range(0, tt, CONV_ROWS):
        for cb in range(CONV_CH // LANES):
            cs = slice(cb * LANES, (cb + 1) * LANES)
            acc = jnp.broadcast_to(cb_ref[:, cs], (CONV_ROWS, LANES))
            for w in range(CONV_WIDTH):
                r0w = r + CONV_OFF + w
                acc = acc + hpad[cb, r0w:r0w + CONV_ROWS, :] * cw_ref[w:w + 1, cs]
            cacc_scr[r:r + CONV_ROWS, cs] = acc
    for cb in range(CONV_CH // LANES):
        hpad[cb, 0:CONV_HALO, :] = hpad[cb, tt:tt + CONV_HALO, :]
    acc = cacc_scr[...]
    mu = jnp.mean(acc, axis=-1, keepdims=True)
    ac = acc - mu
    var = jnp.mean(ac * ac, axis=-1, keepdims=True)
    cn = ac * lax.rsqrt(var + EPS) * clg_ref[...] + clb_ref[...]
    mix_scr[:, 0:CONV_CH] = (cn * _sigmoid(cn)).astype(BF16)

    cq = _rms(p_scr[:, ODD_CQ:ODD_CKV], qn_ref[...]).astype(BF16)
    ckv = _rms(p_scr[:, ODD_CKV:ODD_ROPE], kvn_ref[...]).astype(BF16)
    qf = _dot(cq, wuq_ref[...])
    kvf = _dot(ckv, wukv_ref[...])

    ang = pos_ref[0] * inv_ref[...]
    cos = jnp.cos(ang)
    sin_s = jnp.sin(ang) * sgn_ref[...]
    rot = jnp.concatenate([cos[:, :MLA_ROPE], sin_s[:, MLA_ROPE:]], axis=1)
    kr = p_scr[:, ODD_ROPE:IN_ODD_EXT] * rot
    kr = kr + pltpu.roll(kr, MLA_ROPE, axis=1)
    q_mult = jnp.concatenate([jnp.full((tt, MLA_NOPE), scale, F32), rot * scale], axis=1)

    r0 = pl.multiple_of(t_idx * tt, tt)
    for hd in range(MLA_HEADS):
        q_scr[hd] = (qf[:, hd * Q_SLOT:(hd + 1) * Q_SLOT] * q_mult).astype(BF16)
        kv0 = hd * (MLA_NOPE + MLA_V)
        k_scr[hd, pl.ds(r0, tt), 0:MLA_NOPE] = kvf[:, kv0:kv0 + MLA_NOPE].astype(BF16)
        k_scr[hd, pl.ds(r0, tt), MLA_NOPE:Q_SLOT] = kr.astype(BF16)
        vt_scr[hd, t_idx] = kvf[:, kv0 + MLA_NOPE:kv0 + MLA_NOPE + MLA_V].T.astype(BF16)
        m_scr[hd] = jnp.full((1, tt), NEG_BIG, F32)
        l_scr[hd] = jnp.zeros((1, tt), F32)
        acc_scr[hd] = jnp.zeros((MLA_V, tt), F32)

    heads = range(MLA_HEADS)

    def scores_into(kb, slot):
        k0 = pl.multiple_of(kb * tt, tt)
        for hd in heads:
            s_scr[slot, hd] = _dot_nt(k_scr[hd, pl.ds(k0, tt), :], q_scr[hd])

    def attend(kb, masked):
        sts = [s_scr[kb % 2, hd] for hd in heads]
        if not masked:
            scores_into(kb + 1, (kb + 1) % 2)
        if masked:
            krow = lax.broadcasted_iota(jnp.int32, (tt, tt), 0)
            qcol = lax.broadcasted_iota(jnp.int32, (tt, tt), 1)
            sts = [jnp.where(krow <= qcol, st, NEG_BIG) for st in sts]
        m_olds = [m_scr[hd] for hd in heads]
        m_news = [jnp.maximum(m_olds[hd], jnp.max(sts[hd], axis=0, keepdims=True)) for hd in heads]
        alphas = [jnp.exp(m_olds[hd] - m_news[hd]) for hd in heads]
        prs = [jnp.exp(sts[hd] - m_news[hd]) for hd in heads]
        for hd in heads:
            l_scr[hd] = alphas[hd] * l_scr[hd] + jnp.sum(prs[hd], axis=0, keepdims=True)
            m_scr[hd] = m_news[hd]
        pvs = [_dot(vt_scr[hd, kb], prs[hd].astype(BF16)) for hd in heads]
        for hd in heads:
            acc_scr[hd] = acc_scr[hd] * alphas[hd] + pvs[hd]

    def kv_body(kb, carry):
        attend(kb, False)
        return carry

    scores_into(0, 0)
    lax.fori_loop(0, t_idx, kv_body, 0)
    attend(t_idx, True)
    for hd in range(MLA_HEADS):
        o_t = acc_scr[hd] * (1.0 / l_scr[hd])
        mix_scr[:, CONV_CH + hd * MLA_V:CONV_CH + (hd + 1) * MLA_V] = o_t.T.astype(BF16)

    o_ref[0] = x + _dot(mix_scr[...], wout_ref[...])


def _odd_mixer(x, pos_bc, norm_w, w_in_ext, w_out, conv_w, conv_b, cln_g, cln_b,
               q_norm, w_uq_ext, kv_norm, w_ukv, inv_row, sgn_row):
    bsz, seq, _ = x.shape
    tt = min(ODD_TILE, seq)
    return pl.pallas_call(
        functools.partial(_odd_kernel, tt=tt),
        grid=(bsz, seq // tt),
        in_specs=[
            pl.BlockSpec((1, tt, D_MODEL), lambda b, t: (b, t, 0)),
            pl.BlockSpec((1, tt, 2 * MLA_ROPE), lambda b, t: (b, t, 0)),
            _const_spec((1, D_MODEL)),
            _const_spec((D_MODEL, IN_ODD_EXT)),
            _const_spec((D_MODEL, D_MODEL)),
            _const_spec((CONV_WIDTH, CONV_CH)),
            _const_spec((1, CONV_CH)),
            _const_spec((1, CONV_CH)),
            _const_spec((1, CONV_CH)),
            _const_spec((1, MLA_Q_RANK)),
            _const_spec((MLA_Q_RANK, MLA_HEADS * Q_SLOT)),
            _const_spec((1, MLA_KV_RANK)),
            _const_spec((MLA_KV_RANK, MLA_HEADS * (MLA_NOPE + MLA_V))),
            _const_spec((1, 2 * MLA_ROPE)),
            _const_spec((1, 2 * MLA_ROPE)),
        ],
        out_specs=pl.BlockSpec((1, tt, D_MODEL), lambda b, t: (b, t, 0)),
        out_shape=jax.ShapeDtypeStruct(x.shape, F32),
        scratch_shapes=[
            pltpu.VMEM((tt, IN_ODD_EXT), F32),
            pltpu.VMEM((CONV_CH // LANES, CONV_HALO + tt, LANES), F32),
            pltpu.VMEM((tt, CONV_CH), F32),
            pltpu.VMEM((MLA_HEADS, seq, Q_SLOT), BF16),
            pltpu.VMEM((MLA_HEADS, seq // tt, MLA_V, tt), BF16),
            pltpu.VMEM((MLA_HEADS, tt, Q_SLOT), BF16),
            pltpu.VMEM((MLA_HEADS, 1, tt), F32),
            pltpu.VMEM((MLA_HEADS, 1, tt), F32),
            pltpu.VMEM((MLA_HEADS, MLA_V, tt), F32),
            pltpu.VMEM((2, MLA_HEADS, tt, tt), F32),
            pltpu.VMEM((tt, D_MODEL), BF16),
        ],
        compiler_params=pltpu.CompilerParams(
            dimension_semantics=("arbitrary", "arbitrary"), vmem_limit_bytes=VMEM_LIMIT_BYTES),
        name="odd_mixer",
    )(x, pos_bc, norm_w, w_in_ext, w_out, conv_w, conv_b, cln_g, cln_b,
      q_norm, w_uq_ext, kv_norm, w_ukv, inv_row, sgn_row)


def _swap_halves(w):
    half = w.shape[-1] // 2
    return jnp.concatenate([w[..., half:], w[..., :half]], axis=-1)


def kernel(x, positions, mix_norm, ffn_norm, ffn_gate, ffn_up, ffn_down, w_in_even, w_out_even,
           hgrn_lb_logits, hgrn_gnorm, sgu_ln_g, sgu_ln_b, sgu_w, sgu_b, w_in_odd, w_out_odd,
           conv_w, conv_b, conv_ln_g, conv_ln_b, mla_q_norm, mla_w_uq, mla_kv_norm, mla_w_ukv,
           final_norm):
    bsz, seq, _ = x.shape
    depth = mix_norm.shape[0]
    lower_bounds = jnp.cumsum(jax.nn.softmax(hgrn_lb_logits.astype(F32), axis=0), axis=0)

    inv = 1.0 / (ROPE_THETA ** (jnp.arange(0, MLA_ROPE, 2, dtype=F32) / MLA_ROPE))
    inv_row = jnp.tile(inv, 4)[None, :]
    sgn_row = jnp.concatenate([jnp.ones((MLA_ROPE,), F32), -jnp.ones((MLA_ROPE // 2,), F32),
                               jnp.ones((MLA_ROPE // 2,), F32)])[None, :]
    pos_bc = jnp.broadcast_to(positions.astype(F32)[:, :, None], (bsz, seq, 2 * MLA_ROPE))

    row = lambda v: v.reshape(1, -1).astype(F32)
    for layer in range(depth):
        j = layer // 2
        if layer % 2 == 0:
            sgu_b_bc = jnp.broadcast_to(sgu_b[j][:, :, None], (SGU_GROUPS, SGU_CHUNK, SGU_CH))
            x = _even_layer(x, row(mix_norm[layer]), w_in_even[j].astype(BF16),
                            w_out_even[j].astype(BF16), row(lower_bounds[j]), row(hgrn_gnorm[j]),
                            row(sgu_ln_g[j]), row(sgu_ln_b[j]), sgu_w[j], sgu_b_bc,
                            row(ffn_norm[layer]), ffn_gate[layer].astype(BF16),
                            ffn_up[layer].astype(BF16), ffn_down[layer].astype(BF16))
            continue
        else:
            w_in = w_in_odd[j]
            w_in_ext = jnp.concatenate([w_in, _swap_halves(w_in[:, ODD_ROPE:])], axis=1).astype(BF16)
            wq = mla_w_uq[j].reshape(MLA_Q_RANK, MLA_HEADS, MLA_NOPE + MLA_ROPE)
            wq_ext = jnp.concatenate([wq, _swap_halves(wq[:, :, MLA_NOPE:])], axis=2)
            wq_ext = wq_ext.reshape(MLA_Q_RANK, MLA_HEADS * Q_SLOT).astype(BF16)
            x = _odd_mixer(x, pos_bc, row(mix_norm[layer]), w_in_ext, w_out_odd[j].astype(BF16),
                           conv_w[j], row(conv_b[j]), row(conv_ln_g[j]), row(conv_ln_b[j]),
                           row(mla_q_norm[j]), wq_ext, row(mla_kv_norm[j]),
                           mla_w_ukv[j].astype(BF16), inv_row, sgn_row)
        x2 = _ffn(x.reshape(bsz * seq, D_MODEL), row(ffn_norm[layer]), ffn_gate[layer].astype(BF16),
                  ffn_up[layer].astype(BF16), ffn_down[layer].astype(BF16), row(final_norm),
                  final=(layer == depth - 1))
        x = x2.reshape(bsz, seq, D_MODEL)
    return x
```

```python
import functools
import math

import jax
import jax.numpy as jnp
from jax import lax
from jax.experimental import pallas as pl
from jax.experimental.pallas import tpu as pltpu

F32 = jnp.float32
BF16 = jnp.bfloat16

D_MODEL = 1024
MIX_HALF = D_MODEL // 2
HGRN_HEADS = 4
HGRN_DK = 128
HGRN_DV = MIX_HALF // HGRN_HEADS
HGRN_K = HGRN_HEADS * HGRN_DK
HGRN_V = HGRN_HEADS * HGRN_DV
SGU_GROUPS = 4
SGU_CH = MIX_HALF // SGU_GROUPS
SGU_CHUNK = 128
CONV_CH = MIX_HALF
CONV_WIDTH = 31
MLA_HEADS = 4
MLA_NOPE = 128
MLA_ROPE = 64
MLA_V = 128
MLA_Q_RANK = 384
MLA_KV_RANK = 256
ROPE_THETA = 10000.0
D_FF = -(-8 * D_MODEL // (3 * 256)) * 256
EPS = 1e-6
IN_EVEN = 2 * HGRN_K + 2 * HGRN_V + 2 * MIX_HALF
IN_ODD = 2 * CONV_CH + MLA_Q_RANK + MLA_KV_RANK + MLA_ROPE

LANES = 128
SUBLANES = 8
VMEM_LIMIT_BYTES = 56 * 1024 * 1024

MIX_TILE = 256
ODD_TILE = 512
FFN_TILE = 512
CAST_ROWS = 256
FFN_CHUNKS = ((0, 768), (768, 1536), (1536, 2304), (2304, D_FF))
FFN_UP_BLOCK = 512
FFN_DOWN_BLOCK = 512
HGRN_CHUNK = 64
HGRN_LEVELS = (32, 16, 8, 4)
HGRN_DIAG = 4
CONV_HALO = 32
CONV_OFF = CONV_HALO - (CONV_WIDTH - 1)
CONV_ROWS = 32
NEG_BIG = -1e30


def _rms(x, w):
    return x * lax.rsqrt(jnp.mean(x * x, axis=-1, keepdims=True) + EPS) * w


def _dot(a, b):
    return jnp.dot(a, b, preferred_element_type=F32)


def _dot_nt(a, b):
    return lax.dot_general(a, b, (((1,), (1,)), ((), ())), preferred_element_type=F32)


def _sigmoid(x):
    return 1.0 / (1.0 + jnp.exp(-x))


def _gelu_tanh(x):
    return 0.5 * x * (1.0 + jnp.tanh(math.sqrt(2.0 / math.pi) * (x + 0.044715 * (x * x * x))))


def _const_spec(shape):
    nd = len(shape)
    return pl.BlockSpec(shape, lambda *_: (0,) * nd, pipeline_mode=pl.Buffered(1))


def _layer_spec(layer, shape):
    return pl.BlockSpec((None,) + tuple(shape), lambda *_: (layer, 0, 0),
                        pipeline_mode=pl.Buffered(1))


def _cast_kernel(w_ref, o_ref):
    o_ref[...] = w_ref[...].astype(BF16)


def _to_bf16(w):
    shape = w.shape
    w2 = w.reshape(-1, shape[-1])
    rows, cols = w2.shape
    tr = CAST_ROWS if rows % CAST_ROWS == 0 else rows
    out = pl.pallas_call(
        _cast_kernel,
        grid=(rows // tr,),
        in_specs=[pl.BlockSpec((tr, cols), lambda i: (i, 0))],
        out_specs=pl.BlockSpec((tr, cols), lambda i: (i, 0)),
        out_shape=jax.ShapeDtypeStruct((rows, cols), BF16),
        compiler_params=pltpu.CompilerParams(
            dimension_semantics=("arbitrary",), vmem_limit_bytes=VMEM_LIMIT_BYTES),
        name="cast_bf16",
    )(w2)
    return out.reshape(shape)


class _TrailingFfn:
    def __init__(self, x_prev, nw_ref, wg_ref, wu_ref, wd_ref, act_scr, out_fn):
        self.x, self.wg_ref, self.wu_ref, self.wd_ref = x_prev, wg_ref, wu_ref, wd_ref
        self.act_scr, self.out_fn = act_scr, out_fn
        self.h = _rms(x_prev, nw_ref[...]).astype(BF16)
        up = [(self._up, c0) for c0 in range(0, D_FF, FFN_UP_BLOCK)]
        down = [(self._down, n0) for n0 in range(0, D_MODEL, FFN_DOWN_BLOCK)]
        self.todo = up + down

    def _up(self, c0):
        cols = slice(c0, min(c0 + FFN_UP_BLOCK, D_FF))
        g = _dot(self.h, self.wg_ref[:, cols])
        u = _dot(self.h, self.wu_ref[:, cols])
        self.act_scr[:, cols] = (g * _sigmoid(g) * u).astype(BF16)

    def _down(self, n0):
        cols = slice(n0, n0 + FFN_DOWN_BLOCK)
        self.out_fn(cols, self.x[:, cols] + _dot(self.act_scr[...], self.wd_ref[:, cols]))

    def step(self, n=1):
        for _ in range(min(n, len(self.todo))):
            fn, arg = self.todo.pop(0)
            fn(arg)

    def finish(self):
        self.step(len(self.todo))


def _ffn_kernel(x_ref, nw_ref, wg_ref, wu_ref, wd_ref, fw_ref, o_ref, *, final):
    x = x_ref[...]
    h = _rms(x, nw_ref[...]).astype(BF16)
    acc = x
    for c0, c1 in FFN_CHUNKS:
        g = _dot(h, wg_ref[:, c0:c1])
        u = _dot(h, wu_ref[:, c0:c1])
        a = (g * _sigmoid(g) * u).astype(BF16)
        acc = acc + _dot(a, wd_ref[c0:c1, :])
    if final:
        acc = _rms(acc, fw_ref[...])
    o_ref[...] = acc


def _ffn(x2, norm_w, w_gate, w_up, w_down, final_w, *, layer, final):
    n = x2.shape[0]
    tm = min(FFN_TILE, n)
    return pl.pallas_call(
        functools.partial(_ffn_kernel, final=final),
        grid=(n // tm,),
        in_specs=[
            pl.BlockSpec((tm, D_MODEL), lambda i: (i, 0)),
            _const_spec((1, D_MODEL)),
            _layer_spec(layer, (D_MODEL, D_FF)),
            _layer_spec(layer, (D_MODEL, D_FF)),
            _layer_spec(layer, (D_FF, D_MODEL)),
            _const_spec((1, D_MODEL)),
        ],
        out_specs=pl.BlockSpec((tm, D_MODEL), lambda i: (i, 0)),
        out_shape=jax.ShapeDtypeStruct((n, D_MODEL), F32),
        compiler_params=pltpu.CompilerParams(
            dimension_semantics=("arbitrary",), vmem_limit_bytes=VMEM_LIMIT_BYTES),
        name="ffn_final" if final else "ffn",
    )(x2, norm_w, w_gate, w_up, w_down, final_w)


def _even_kernel(x_ref, nw_ref, win_ref, wout_ref, lb_ref, gn_ref, lng_ref, lnb_ref,
                 sw_ref, sb_ref, fnw_ref, wg_ref, wu_ref, wd_ref, o_ref,
                 p_scr, st_scr, kpad, bpad, msk_scr, mix_scr, xm_scr, act_scr, *, tt, nt, steps):
    s_idx = pl.program_id(0)
    t_idx = lax.rem(jnp.minimum(s_idx, steps - 1), nt)
    slot = lax.rem(s_idx, 2)
    C, PAD = HGRN_CHUNK, SUBLANES
    heads = range(HGRN_HEADS)

    @pl.when(s_idx == 0)
    def _():
        xm_scr[1] = jnp.zeros((tt, D_MODEL), F32)

    @pl.when(t_idx == 0)
    def _():
        st_scr[...] = jnp.zeros_like(st_scr)
        kpad[:, 0:PAD, :] = jnp.zeros((HGRN_HEADS, PAD, HGRN_DK), F32)
        bpad[:, 0:PAD, :] = jnp.zeros((HGRN_HEADS, PAD, HGRN_DK), F32)

    def ffn_out(cols, val):
        o_ref[0, :, cols] = val

    ffn = _TrailingFfn(xm_scr[1 - slot], fnw_ref, wg_ref, wu_ref, wd_ref, act_scr, ffn_out)
    x = x_ref[0]
    h = _rms(x, nw_ref[...]).astype(BF16)
    p_scr[...] = _dot(h, win_ref[...])

    lb = lb_ref[...]
    ti = lax.broadcasted_iota(jnp.int32, (C, C), 0)
    si = lax.broadcasted_iota(jnp.int32, (C, C), 1)
    tril_c = jnp.where(ti >= si, 1.0, 0.0).astype(BF16)
    tril3 = jnp.concatenate([tril_c, tril_c, tril_c], axis=1)
    xr = jnp.bitwise_xor(ti, si)
    for li, hs in enumerate(HGRN_LEVELS):
        own = jnp.logical_and(lax.shift_right_logical(xr, hs.bit_length() - 1) == 1, ti > si)
        msk_scr[li] = jnp.where(own, 1.0, 0.0)
    for d in range(HGRN_DIAG):
        own = jnp.logical_and(ti - si == d, xr < HGRN_DIAG)
        msk_scr[len(HGRN_LEVELS) + d] = jnp.where(own, 1.0, 0.0)

    for c in range(tt // C):
        ffn.step()
        rows = slice(c * C, (c + 1) * C)
        q_all = p_scr[rows, 0:HGRN_K]
        f_pre = p_scr[rows, HGRN_K:2 * HGRN_K]
        iv_all = p_scr[rows, 2 * HGRN_K:2 * HGRN_K + HGRN_V]
        g_all = p_scr[rows, 2 * HGRN_K + HGRN_V:2 * HGRN_K + 2 * HGRN_V]
        f = lb + (1.0 - lb) * _sigmoid(f_pre)
        lf = jnp.log(f)
        lf_hi = lf.astype(BF16)
        lf_r = lf - lf_hi.astype(F32)
        lf_mid = lf_r.astype(BF16)
        lf_lo = (lf_r - lf_mid.astype(F32)).astype(BF16)
        b_all = _dot(tril3, jnp.concatenate([lf_hi, lf_mid, lf_lo], axis=0))
        k_all = 1.0 - f
        for hd in heads:
            kpad[hd, PAD:PAD + C, :] = k_all[:, hd * HGRN_DK:(hd + 1) * HGRN_DK]
            bpad[hd, PAD:PAD + C, :] = b_all[:, hd * HGRN_DK:(hd + 1) * HGRN_DK]

        def rows_of(ref, start, size):
            return jnp.concatenate([ref[hd, start:start + size, :] for hd in heads], axis=1)

        b_last = rows_of(bpad, PAD + C - 1, 1)
        q_inter = (q_all * jnp.exp(b_all)).astype(BF16)
        k_dec = (k_all * jnp.exp(b_last - b_all)).astype(BF16)
        dec_row = jnp.exp(b_last)
        iv_bf = iv_all.astype(BF16)
        gate_all = g_all * _sigmoid(g_all)

        lev = []
        for hs in HGRN_LEVELS:
            pivots = [blk * 2 * hs + hs - 1 for blk in range(C // (2 * hs))]
            piv = jnp.concatenate(
                [jnp.broadcast_to(rows_of(bpad, PAD + p, 1), (2 * hs, HGRN_K)) for p in pivots],
                axis=0)
            e = jnp.exp(-jnp.abs(b_all - piv))
            lev.append(((q_all * e).astype(BF16), (k_all * e).astype(BF16)))
        diag = [q_all * k_all]
        for d in range(1, HGRN_DIAG):
            k_sh = rows_of(kpad, PAD - d, C)
            b_sh = rows_of(bpad, PAD - d, C)
            diag.append(q_all * k_sh * jnp.exp(b_all - b_sh))

        for hd in heads:
            if hd % 2 == 0:
                ffn.step()
            sl = slice(hd * HGRN_DK, (hd + 1) * HGRN_DK)
            vsl = slice(hd * HGRN_DV, (hd + 1) * HGRN_DV)
            sc = jnp.zeros((C, C), F32)
            for li in range(len(HGRN_LEVELS)):
                sc = sc + _dot_nt(lev[li][0][:, sl], lev[li][1][:, sl]) * msk_scr[li]
            for d in range(HGRN_DIAG):
                sc = sc + (jnp.sum(diag[d][:, sl], axis=-1, keepdims=True)
                           * msk_scr[len(HGRN_LEVELS) + d])
            st = st_scr[hd]
            o = _dot_nt(q_inter[:, sl], st.astype(BF16)) + _dot(sc.astype(BF16), iv_bf[:, vsl])
            st_scr[hd] = st * dec_row[:, sl] + _dot(iv_all[:, vsl].T.astype(BF16), k_dec[:, sl])
            on = _rms(o, gn_ref[:, vsl])
            mix_scr[rows, vsl] = (on * gate_all[:, vsl]).astype(BF16)

    prow = lax.broadcasted_iota(jnp.int32, (SGU_CHUNK, SGU_CHUNK), 0)
    pcol = lax.broadcasted_iota(jnp.int32, (SGU_CHUNK, SGU_CHUNK), 1)
    w_causal = [jnp.where(prow >= pcol, sw_ref[gi], 0.0).astype(BF16) for gi in range(SGU_GROUPS)]
    for n in range(tt // SGU_CHUNK):
        rows = slice(n * SGU_CHUNK, (n + 1) * SGU_CHUNK)
        for gi in range(SGU_GROUPS):
            if gi % 2 == 0:
                ffn.step()
            csl = slice(gi * SGU_CH, (gi + 1) * SGU_CH)
            u = _gelu_tanh(p_scr[rows, 2 * HGRN_K + 2 * HGRN_V + gi * SGU_CH:
                                 2 * HGRN_K + 2 * HGRN_V + (gi + 1) * SGU_CH])
            v = _gelu_tanh(p_scr[rows, 2 * HGRN_K + 2 * HGRN_V + MIX_HALF + gi * SGU_CH:
                                 2 * HGRN_K + 2 * HGRN_V + MIX_HALF + (gi + 1) * SGU_CH])
            mu = jnp.mean(v, axis=-1, keepdims=True)
            vc = v - mu
            var = jnp.mean(vc * vc, axis=-1, keepdims=True)
            vn = vc * lax.rsqrt(var + EPS) * lng_ref[:, csl] + lnb_ref[:, csl]
            z = _dot(w_causal[gi], vn.astype(BF16)) + sb_ref[gi]
            mix_scr[rows, HGRN_V + gi * SGU_CH:HGRN_V + (gi + 1) * SGU_CH] = (u * z).astype(BF16)

    xm_scr[slot] = x + _dot(mix_scr[...], wout_ref[...])
    ffn.finish()


def _tile_index_maps(nt, steps):
    def mixer_tile(s):
        m = jnp.minimum(s, steps - 1)
        return m // nt, m % nt, 0

    def ffn_tile(s):
        f = jnp.maximum(s - 1, 0)
        return f // nt, f % nt, 0

    return mixer_tile, ffn_tile


def _even_layer(x, norm_w, w_in, w_out, lb, gnorm, ln_g, ln_b, sgu_w, sgu_b_bc,
                ffn_norm_w, w_gate, w_up, w_down, *, j, layer):
    bsz, seq, _ = x.shape
    tt = min(MIX_TILE, seq)
    nt = seq // tt
    steps = bsz * nt
    mixer_tile, ffn_tile = _tile_index_maps(nt, steps)
    return pl.pallas_call(
        functools.partial(_even_kernel, tt=tt, nt=nt, steps=steps),
        grid=(steps + 1,),
        in_specs=[
            pl.BlockSpec((1, tt, D_MODEL), mixer_tile),
            _const_spec((1, D_MODEL)),
            _layer_spec(j, (D_MODEL, IN_EVEN)),
            _layer_spec(j, (D_MODEL, D_MODEL)),
            _const_spec((1, HGRN_K)),
            _const_spec((1, HGRN_V)),
            _const_spec((1, MIX_HALF)),
            _const_spec((1, MIX_HALF)),
            _const_spec((SGU_GROUPS, SGU_CHUNK, SGU_CHUNK)),
            _const_spec((SGU_GROUPS, SGU_CHUNK, SGU_CH)),
            _const_spec((1, D_MODEL)),
            _layer_spec(layer, (D_MODEL, D_FF)),
            _layer_spec(layer, (D_MODEL, D_FF)),
            _layer_spec(layer, (D_FF, D_MODEL)),
        ],
        out_specs=pl.BlockSpec((1, tt, D_MODEL), ffn_tile),
        out_shape=jax.ShapeDtypeStruct(x.shape, F32),
        scratch_shapes=[
            pltpu.VMEM((tt, IN_EVEN), F32),
            pltpu.VMEM((HGRN_HEADS, HGRN_DV, HGRN_DK), F32),
            pltpu.VMEM((HGRN_HEADS, SUBLANES + HGRN_CHUNK, HGRN_DK), F32),
            pltpu.VMEM((HGRN_HEADS, SUBLANES + HGRN_CHUNK, HGRN_DK), F32),
            pltpu.VMEM((len(HGRN_LEVELS) + HGRN_DIAG, HGRN_CHUNK, HGRN_CHUNK), F32),
            pltpu.VMEM((tt, D_MODEL), BF16),
            pltpu.VMEM((2, tt, D_MODEL), F32),
            pltpu.VMEM((tt, D_FF), BF16),
        ],
        compiler_params=pltpu.CompilerParams(
            dimension_semantics=("arbitrary",), vmem_limit_bytes=VMEM_LIMIT_BYTES),
        name="even_layer",
    )(x, norm_w, w_in, w_out, lb, gnorm, ln_g, ln_b, sgu_w, sgu_b_bc,
      ffn_norm_w, w_gate, w_up, w_down)


ODD_CQ = 2 * CONV_CH
ODD_CKV = ODD_CQ + MLA_Q_RANK
ODD_ROPE = ODD_CKV + MLA_KV_RANK
IN_ODD_EXT = ODD_ROPE + 2 * MLA_ROPE
Q_SLOT = MLA_NOPE + 2 * MLA_ROPE


def _odd_kernel(x_ref, pos_ref, nw_ref, win_ref, wout_ref, cw_ref, cb_ref, clg_ref, clb_ref,
                qn_ref, wuq_ref, kvn_ref, wukv_ref, inv_ref, sgn_ref, o_ref,
                p_scr, hpad, cacc_scr, k_scr, vt_scr, q_scr, m_scr, l_scr, acc_scr, s_scr, mix_scr,
                *, tt):
    t_idx = pl.program_id(1)
    scale = (MLA_NOPE + MLA_ROPE) ** -0.5

    @pl.when(t_idx == 0)
    def _():
        hpad[:, 0:CONV_HALO, :] = jnp.zeros((CONV_CH // LANES, CONV_HALO, LANES), F32)

    x = x_ref[0]
    h = _rms(x, nw_ref[...]).astype(BF16)
    p_scr[...] = _dot(h, win_ref[...])

    for cb in range(CONV_CH // LANES):
        cs = slice(cb * LANES, (cb + 1) * LANES)
        a = p_scr[:, cs]
        gate = p_scr[:, CONV_CH + cb * LANES:CONV_CH + (cb + 1) * LANES]
        hpad[cb, CONV_HALO:CONV_HALO + tt, :] = a * _sigmoid(gate)
    for r in range(0, tt, CONV_ROWS):
        for cb in range(CONV_CH // LANES):
            cs = slice(cb * LANES, (cb + 1) * LANES)
            acc = jnp.broadcast_to(cb_ref[:, cs], (CONV_ROWS, LANES))
            for w in range(CONV_WIDTH):
                r0w = r + CONV_OFF + w
                acc = acc + hpad[cb, r0w:r0w + CONV_ROWS, :] * cw_ref[w:w + 1, cs]
            cacc_scr[r:r + CONV_ROWS, cs] = acc
    for cb in range(CONV_CH // LANES):
        hpad[cb, 0:CONV_HALO, :] = hpad[cb, tt:tt + CONV_HALO, :]
    acc = cacc_scr[...]
    mu = jnp.mean(acc, axis=-1, keepdims=True)
    ac = acc - mu
    var = jnp.mean(ac * ac, axis=-1, keepdims=True)
    cn = ac * lax.rsqrt(var + EPS) * clg_ref[...] + clb_ref[...]
    mix_scr[:, 0:CONV_CH] = (cn * _sigmoid(cn)).astype(BF16)

    cq = _rms(p_scr[:, ODD_CQ:ODD_CKV], qn_ref[...]).astype(BF16)
    ckv = _rms(p_scr[:, ODD_CKV:ODD_ROPE], kvn_ref[...]).astype(BF16)
    qf = _dot(cq, wuq_ref[...])
    kvf = _dot(ckv, wukv_ref[...])

    ang = pos_ref[0] * inv_ref[...]
    cos = jnp.cos(ang)
    sin_s = jnp.sin(ang) * sgn_ref[...]
    rot = jnp.concatenate([cos[:, :MLA_ROPE], sin_s[:, MLA_ROPE:]], axis=1)
    kr = p_scr[:, ODD_ROPE:IN_ODD_EXT] * rot
    kr = kr + pltpu.roll(kr, MLA_ROPE, axis=1)
    q_mult = jnp.concatenate([jnp.full((tt, MLA_NOPE), scale, F32), rot * scale], axis=1)

    r0 = pl.multiple_of(t_idx * tt, tt)
    for hd in range(MLA_HEADS):
        q_scr[hd] = (qf[:, hd * Q_SLOT:(hd + 1) * Q_SLOT] * q_mult).astype(BF16)
        kv0 = hd * (MLA_NOPE + MLA_V)
        k_scr[hd, pl.ds(r0, tt), 0:MLA_NOPE] = kvf[:, kv0:kv0 + MLA_NOPE].astype(BF16)
        k_scr[hd, pl.ds(r0, tt), MLA_NOPE:Q_SLOT] = kr.astype(BF16)
        vt_scr[hd, t_idx] = kvf[:, kv0 + MLA_NOPE:kv0 + MLA_NOPE + MLA_V].T.astype(BF16)
        m_scr[hd] = jnp.full((1, tt), NEG_BIG, F32)
        l_scr[hd] = jnp.zeros((1, tt), F32)
        acc_scr[hd] = jnp.zeros((MLA_V, tt), F32)

    heads = range(MLA_HEADS)

    def scores_into(kb, slot):
        k0 = pl.multiple_of(kb * tt, tt)
        for hd in heads:
            s_scr[slot, hd] = _dot_nt(k_scr[hd, pl.ds(k0, tt), :], q_scr[hd])

    def attend(kb, masked):
        sts = [s_scr[kb % 2, hd] for hd in heads]
        if not masked:
            scores_into(kb + 1, (kb + 1) % 2)
        if masked:
            krow = lax.broadcasted_iota(jnp.int32, (tt, tt), 0)
            qcol = lax.broadcasted_iota(jnp.int32, (tt, tt), 1)
            sts = [jnp.where(krow <= qcol, st, NEG_BIG) for st in sts]
        m_olds = [m_scr[hd] for hd in heads]
        m_news = [jnp.maximum(m_olds[hd], jnp.max(sts[hd], axis=0, keepdims=True)) for hd in heads]
        alphas = [jnp.exp(m_olds[hd] - m_news[hd]) for hd in heads]
        prs = [jnp.exp(sts[hd] - m_news[hd]) for hd in heads]
        for hd in heads:
            l_scr[hd] = alphas[hd] * l_scr[hd] + jnp.sum(prs[hd], axis=0, keepdims=True)
            m_scr[hd] = m_news[hd]
        pvs = [_dot(vt_scr[hd, kb], prs[hd].astype(BF16)) for hd in heads]
        for hd in heads:
            acc_scr[hd] = acc_scr[hd] * alphas[hd] + pvs[hd]

    def kv_body(kb, carry):
        attend(kb, False)
        return carry

    scores_into(0, 0)
    lax.fori_loop(0, t_idx, kv_body, 0)
    attend(t_idx, True)
    for hd in range(MLA_HEADS):
        o_t = acc_scr[hd] * (1.0 / l_scr[hd])
        mix_scr[:, CONV_CH + hd * MLA_V:CONV_CH + (hd + 1) * MLA_V] = o_t.T.astype(BF16)

    o_ref[0] = x + _dot(mix_scr[...], wout_ref[...])


def _odd_mixer(x, pos_bc, norm_w, w_in_ext, w_out, conv_w, conv_b, cln_g, cln_b,
               q_norm, w_uq_ext, kv_norm, w_ukv, inv_row, sgn_row, *, j):
    bsz, seq, _ = x.shape
    tt = min(ODD_TILE, seq)
    return pl.pallas_call(
        functools.partial(_odd_kernel, tt=tt),
        grid=(bsz, seq // tt),
        in_specs=[
            pl.BlockSpec((1, tt, D_MODEL), lambda b, t: (b, t, 0)),
            pl.BlockSpec((1, tt, 2 * MLA_ROPE), lambda b, t: (b, t, 0)),
            _const_spec((1, D_MODEL)),
            _layer_spec(j, (D_MODEL, IN_ODD_EXT)),
            _layer_spec(j, (D_MODEL, D_MODEL)),
            _const_spec((CONV_WIDTH, CONV_CH)),
            _const_spec((1, CONV_CH)),
            _const_spec((1, CONV_CH)),
            _const_spec((1, CONV_CH)),
            _const_spec((1, MLA_Q_RANK)),
            _layer_spec(j, (MLA_Q_RANK, MLA_HEADS * Q_SLOT)),
            _const_spec((1, MLA_KV_RANK)),
            _layer_spec(j, (MLA_KV_RANK, MLA_HEADS * (MLA_NOPE + MLA_V))),
            _const_spec((1, 2 * MLA_ROPE)),
            _const_spec((1, 2 * MLA_ROPE)),
        ],
        out_specs=pl.BlockSpec((1, tt, D_MODEL), lambda b, t: (b, t, 0)),
        out_shape=jax.ShapeDtypeStruct(x.shape, F32),
        scratch_shapes=[
            pltpu.VMEM((tt, IN_ODD_EXT), F32),
            pltpu.VMEM((CONV_CH // LANES, CONV_HALO + tt, LANES), F32),
            pltpu.VMEM((tt, CONV_CH), F32),
            pltpu.VMEM((MLA_HEADS, seq, Q_SLOT), BF16),
            pltpu.VMEM((MLA_HEADS, seq // tt, MLA_V, tt), BF16),
            pltpu.VMEM((MLA_HEADS, tt, Q_SLOT), BF16),
            pltpu.VMEM((MLA_HEADS, 1, tt), F32),
            pltpu.VMEM((MLA_HEADS, 1, tt), F32),
            pltpu.VMEM((MLA_HEADS, MLA_V, tt), F32),
            pltpu.VMEM((2, MLA_HEADS, tt, tt), F32),
            pltpu.VMEM((tt, D_MODEL), BF16),
        ],
        compiler_params=pltpu.CompilerParams(
            dimension_semantics=("arbitrary", "arbitrary"), vmem_limit_bytes=VMEM_LIMIT_BYTES),
        name="odd_mixer",
    )(x, pos_bc, norm_w, w_in_ext, w_out, conv_w, conv_b, cln_g, cln_b,
      q_norm, w_uq_ext, kv_norm, w_ukv, inv_row, sgn_row)


def _swap_halves(w):
    half = w.shape[-1] // 2
    return jnp.concatenate([w[..., half:], w[..., :half]], axis=-1)


def kernel(x, positions, mix_norm, ffn_norm, ffn_gate, ffn_up, ffn_down, w_in_even, w_out_even,
           hgrn_lb_logits, hgrn_gnorm, sgu_ln_g, sgu_ln_b, sgu_w, sgu_b, w_in_odd, w_out_odd,
           conv_w, conv_b, conv_ln_g, conv_ln_b, mla_q_norm, mla_w_uq, mla_kv_norm, mla_w_ukv,
           final_norm):
    bsz, seq, _ = x.shape
    depth = mix_norm.shape[0]
    lower_bounds = jnp.cumsum(jax.nn.softmax(hgrn_lb_logits.astype(F32), axis=0), axis=0)

    inv = 1.0 / (ROPE_THETA ** (jnp.arange(0, MLA_ROPE, 2, dtype=F32) / MLA_ROPE))
    inv_row = jnp.tile(inv, 4)[None, :]
    sgn_row = jnp.concatenate([jnp.ones((MLA_ROPE,), F32), -jnp.ones((MLA_ROPE // 2,), F32),
                               jnp.ones((MLA_ROPE // 2,), F32)])[None, :]
    pos_bc = jnp.broadcast_to(positions.astype(F32)[:, :, None], (bsz, seq, 2 * MLA_ROPE))

    gate_bf, up_bf, down_bf = _to_bf16(ffn_gate), _to_bf16(ffn_up), _to_bf16(ffn_down)
    w_in_even_bf, w_out_even_bf = _to_bf16(w_in_even), _to_bf16(w_out_even)
    w_out_odd_bf, w_ukv_bf = _to_bf16(w_out_odd), _to_bf16(mla_w_ukv)
    w_in_odd_bf = _to_bf16(
        jnp.concatenate([w_in_odd, _swap_halves(w_in_odd[:, :, ODD_ROPE:])], axis=2))
    wq = mla_w_uq.reshape(-1, MLA_Q_RANK, MLA_HEADS, MLA_NOPE + MLA_ROPE)
    wq_ext = jnp.concatenate([wq, _swap_halves(wq[..., MLA_NOPE:])], axis=3)
    w_uq_bf = _to_bf16(wq_ext.reshape(-1, MLA_Q_RANK, MLA_HEADS * Q_SLOT))

    row = lambda v: v.reshape(1, -1).astype(F32)
    for layer in range(depth):
        j = layer // 2
        if layer % 2 == 0:
            sgu_b_bc = jnp.broadcast_to(sgu_b[j][:, :, None], (SGU_GROUPS, SGU_CHUNK, SGU_CH))
            x = _even_layer(x, row(mix_norm[layer]), w_in_even_bf, w_out_even_bf,
                            row(lower_bounds[j]), row(hgrn_gnorm[j]), row(sgu_ln_g[j]),
                            row(sgu_ln_b[j]), sgu_w[j], sgu_b_bc, row(ffn_norm[layer]),
                            gate_bf, up_bf, down_bf, j=j, layer=layer)
            continue
        x = _odd_mixer(x, pos_bc, row(mix_norm[layer]), w_in_odd_bf, w_out_odd_bf,
                       conv_w[j], row(conv_b[j]), row(conv_ln_g[j]), row(conv_ln_b[j]),
                       row(mla_q_norm[j]), w_uq_bf, row(mla_kv_norm[j]), w_ukv_bf,
                       inv_row, sgn_row, j=j)
        x2 = _ffn(x.reshape(bsz * seq, D_MODEL), row(ffn_norm[layer]), gate_bf, up_bf, down_bf,
                  row(final_norm), layer=layer, final=(layer == depth - 1))
        x = x2.reshape(bsz, seq, D_MODEL)
    return x
```

```python
import functools
import math

import jax
import jax.numpy as jnp
from jax import lax
from jax.experimental import pallas as pl
from jax.experimental.pallas import tpu as pltpu

F32 = jnp.float32
BF16 = jnp.bfloat16

D_MODEL = 1024
MIX_HALF = D_MODEL // 2
HGRN_HEADS = 4
HGRN_DK = 128
HGRN_DV = MIX_HALF // HGRN_HEADS
HGRN_K = HGRN_HEADS * HGRN_DK
HGRN_V = HGRN_HEADS * HGRN_DV
SGU_GROUPS = 4
SGU_CH = MIX_HALF // SGU_GROUPS
SGU_CHUNK = 128
CONV_CH = MIX_HALF
CONV_WIDTH = 31
MLA_HEADS = 4
MLA_NOPE = 128
MLA_ROPE = 64
MLA_V = 128
MLA_Q_RANK = 384
MLA_KV_RANK = 256
ROPE_THETA = 10000.0
D_FF = -(-8 * D_MODEL // (3 * 256)) * 256
EPS = 1e-6
IN_EVEN = 2 * HGRN_K + 2 * HGRN_V + 2 * MIX_HALF
IN_ODD = 2 * CONV_CH + MLA_Q_RANK + MLA_KV_RANK + MLA_ROPE

LANES = 128
SUBLANES = 8
VMEM_LIMIT_BYTES = 56 * 1024 * 1024

MIX_TILE = 256
ODD_TILE = 512
FFN_TILE = 512
FFN_CHUNKS = ((0, 768), (768, 1536), (1536, 2304), (2304, D_FF))
FFN_UP_BLOCK = 512
FFN_DOWN_BLOCK = 512
HGRN_CHUNK = 64
HGRN_LEVELS = (32, 16, 8, 4)
HGRN_DIAG = 4
CONV_HALO = 32
CONV_OFF = CONV_HALO - (CONV_WIDTH - 1)
CONV_ROWS = 32
NEG_BIG = -1e30


def _rms(x, w):
    return x * lax.rsqrt(jnp.mean(x * x, axis=-1, keepdims=True) + EPS) * w


def _dot(a, b):
    return jnp.dot(a, b, preferred_element_type=F32)


def _dot_nt(a, b):
    return lax.dot_general(a, b, (((1,), (1,)), ((), ())), preferred_element_type=F32)


def _sigmoid(x):
    return 1.0 / (1.0 + jnp.exp(-x))


def _gelu_tanh(x):
    return 0.5 * x * (1.0 + jnp.tanh(math.sqrt(2.0 / math.pi) * (x + 0.044715 * (x * x * x))))


def _const_spec(shape):
    nd = len(shape)
    return pl.BlockSpec(shape, lambda *_: (0,) * nd, pipeline_mode=pl.Buffered(1))


class _TrailingFfn:
    def __init__(self, x_prev, nw_ref, wg_ref, wu_ref, wd_ref, act_scr, out_fn):
        self.x, self.wg_ref, self.wu_ref, self.wd_ref = x_prev, wg_ref, wu_ref, wd_ref
        self.act_scr, self.out_fn = act_scr, out_fn
        self.h = _rms(x_prev, nw_ref[...]).astype(BF16)
        up = [(self._up, c0) for c0 in range(0, D_FF, FFN_UP_BLOCK)]
        down = [(self._down, n0) for n0 in range(0, D_MODEL, FFN_DOWN_BLOCK)]
        self.todo = up + down

    def _up(self, c0):
        cols = slice(c0, min(c0 + FFN_UP_BLOCK, D_FF))
        g = _dot(self.h, self.wg_ref[:, cols])
        u = _dot(self.h, self.wu_ref[:, cols])
        self.act_scr[:, cols] = (g * _sigmoid(g) * u).astype(BF16)

    def _down(self, n0):
        cols = slice(n0, n0 + FFN_DOWN_BLOCK)
        self.out_fn(cols, self.x[:, cols] + _dot(self.act_scr[...], self.wd_ref[:, cols]))

    def step(self, n=1):
        for _ in range(min(n, len(self.todo))):
            fn, arg = self.todo.pop(0)
            fn(arg)

    def finish(self):
        self.step(len(self.todo))


def _ffn_kernel(x_ref, nw_ref, wg_ref, wu_ref, wd_ref, fw_ref, o_ref, *, final):
    x = x_ref[...]
    h = _rms(x, nw_ref[...]).astype(BF16)
    acc = x
    for c0, c1 in FFN_CHUNKS:
        g = _dot(h, wg_ref[:, c0:c1])
        u = _dot(h, wu_ref[:, c0:c1])
        a = (g * _sigmoid(g) * u).astype(BF16)
        acc = acc + _dot(a, wd_ref[c0:c1, :])
    if final:
        acc = _rms(acc, fw_ref[...])
    o_ref[...] = acc


def _ffn(x2, norm_w, w_gate, w_up, w_down, final_w, *, final):
    n = x2.shape[0]
    tm = min(FFN_TILE, n)
    return pl.pallas_call(
        functools.partial(_ffn_kernel, final=final),
        grid=(n // tm,),
        in_specs=[
            pl.BlockSpec((tm, D_MODEL), lambda i: (i, 0)),
            _const_spec((1, D_MODEL)),
            _const_spec((D_MODEL, D_FF)),
            _const_spec((D_MODEL, D_FF)),
            _const_spec((D_FF, D_MODEL)),
            _const_spec((1, D_MODEL)),
        ],
        out_specs=pl.BlockSpec((tm, D_MODEL), lambda i: (i, 0)),
        out_shape=jax.ShapeDtypeStruct((n, D_MODEL), F32),
        compiler_params=pltpu.CompilerParams(
            dimension_semantics=("arbitrary",), vmem_limit_bytes=VMEM_LIMIT_BYTES),
        name="ffn_final" if final else "ffn",
    )(x2, norm_w, w_gate, w_up, w_down, final_w)


def _even_kernel(x_ref, nw_ref, win_ref, wout_ref, lb_ref, gn_ref, lng_ref, lnb_ref,
                 sw_ref, sb_ref, fnw_ref, wg_ref, wu_ref, wd_ref, o_ref,
                 p_scr, st_scr, kpad, bpad, msk_scr, mix_scr, xm_scr, act_scr, *, tt, nt, steps):
    s_idx = pl.program_id(0)
    t_idx = lax.rem(jnp.minimum(s_idx, steps - 1), nt)
    slot = lax.rem(s_idx, 2)
    C, PAD = HGRN_CHUNK, SUBLANES
    heads = range(HGRN_HEADS)

    @pl.when(s_idx == 0)
    def _():
        xm_scr[1] = jnp.zeros((tt, D_MODEL), F32)

    @pl.when(t_idx == 0)
    def _():
        st_scr[...] = jnp.zeros_like(st_scr)
        kpad[:, 0:PAD, :] = jnp.zeros((HGRN_HEADS, PAD, HGRN_DK), F32)
        bpad[:, 0:PAD, :] = jnp.zeros((HGRN_HEADS, PAD, HGRN_DK), F32)

    def ffn_out(cols, val):
        o_ref[0, :, cols] = val

    ffn = _TrailingFfn(xm_scr[1 - slot], fnw_ref, wg_ref, wu_ref, wd_ref, act_scr, ffn_out)
    x = x_ref[0]
    h = _rms(x, nw_ref[...]).astype(BF16)
    p_scr[...] = _dot(h, win_ref[...])

    lb = lb_ref[...]
    ti = lax.broadcasted_iota(jnp.int32, (C, C), 0)
    si = lax.broadcasted_iota(jnp.int32, (C, C), 1)
    tril_c = jnp.where(ti >= si, 1.0, 0.0).astype(BF16)
    tril3 = jnp.concatenate([tril_c, tril_c, tril_c], axis=1)
    xr = jnp.bitwise_xor(ti, si)
    for li, hs in enumerate(HGRN_LEVELS):
        own = jnp.logical_and(lax.shift_right_logical(xr, hs.bit_length() - 1) == 1, ti > si)
        msk_scr[li] = jnp.where(own, 1.0, 0.0)
    for d in range(HGRN_DIAG):
        own = jnp.logical_and(ti - si == d, xr < HGRN_DIAG)
        msk_scr[len(HGRN_LEVELS) + d] = jnp.where(own, 1.0, 0.0)

    for c in range(tt // C):
        ffn.step()
        rows = slice(c * C, (c + 1) * C)
        q_all = p_scr[rows, 0:HGRN_K]
        f_pre = p_scr[rows, HGRN_K:2 * HGRN_K]
        iv_all = p_scr[rows, 2 * HGRN_K:2 * HGRN_K + HGRN_V]
        g_all = p_scr[rows, 2 * HGRN_K + HGRN_V:2 * HGRN_K + 2 * HGRN_V]
        f = lb + (1.0 - lb) * _sigmoid(f_pre)
        lf = jnp.log(f)
        lf_hi = lf.astype(BF16)
        lf_r = lf - lf_hi.astype(F32)
        lf_mid = lf_r.astype(BF16)
        lf_lo = (lf_r - lf_mid.astype(F32)).astype(BF16)
        b_all = _dot(tril3, jnp.concatenate([lf_hi, lf_mid, lf_lo], axis=0))
        k_all = 1.0 - f
        for hd in heads:
            kpad[hd, PAD:PAD + C, :] = k_all[:, hd * HGRN_DK:(hd + 1) * HGRN_DK]
            bpad[hd, PAD:PAD + C, :] = b_all[:, hd * HGRN_DK:(hd + 1) * HGRN_DK]

        def rows_of(ref, start, size):
            return jnp.concatenate([ref[hd, start:start + size, :] for hd in heads], axis=1)

        b_last = rows_of(bpad, PAD + C - 1, 1)
        q_inter = (q_all * jnp.exp(b_all)).astype(BF16)
        k_dec = (k_all * jnp.exp(b_last - b_all)).astype(BF16)
        dec_row = jnp.exp(b_last)
        iv_bf = iv_all.astype(BF16)
        gate_all = g_all * _sigmoid(g_all)

        lev = []
        for hs in HGRN_LEVELS:
            pivots = [blk * 2 * hs + hs - 1 for blk in range(C // (2 * hs))]
            piv = jnp.concatenate(
                [jnp.broadcast_to(rows_of(bpad, PAD + p, 1), (2 * hs, HGRN_K)) for p in pivots],
                axis=0)
            e = jnp.exp(-jnp.abs(b_all - piv))
            lev.append(((q_all * e).astype(BF16), (k_all * e).astype(BF16)))
        diag = [q_all * k_all]
        for d in range(1, HGRN_DIAG):
            k_sh = rows_of(kpad, PAD - d, C)
            b_sh = rows_of(bpad, PAD - d, C)
            diag.append(q_all * k_sh * jnp.exp(b_all - b_sh))

        for hd in heads:
            if hd % 2 == 0:
                ffn.step()
            sl = slice(hd * HGRN_DK, (hd + 1) * HGRN_DK)
            vsl = slice(hd * HGRN_DV, (hd + 1) * HGRN_DV)
            sc = jnp.zeros((C, C), F32)
            for li in range(len(HGRN_LEVELS)):
                sc = sc + _dot_nt(lev[li][0][:, sl], lev[li][1][:, sl]) * msk_scr[li]
            for d in range(HGRN_DIAG):
                sc = sc + (jnp.sum(diag[d][:, sl], axis=-1, keepdims=True)
                           * msk_scr[len(HGRN_LEVELS) + d])
            st = st_scr[hd]
            o = _dot_nt(q_inter[:, sl], st.astype(BF16)) + _dot(sc.astype(BF16), iv_bf[:, vsl])
            st_scr[hd] = st * dec_row[:, sl] + _dot(iv_all[:, vsl].T.astype(BF16), k_dec[:, sl])
            on = _rms(o, gn_ref[:, vsl])
            mix_scr[rows, vsl] = (on * gate_all[:, vsl]).astype(BF16)

    prow = lax.broadcasted_iota(jnp.int32, (SGU_CHUNK, SGU_CHUNK), 0)
    pcol = lax.broadcasted_iota(jnp.int32, (SGU_CHUNK, SGU_CHUNK), 1)
    w_causal = [jnp.where(prow >= pcol, sw_ref[gi], 0.0).astype(BF16) for gi in range(SGU_GROUPS)]
    for n in range(tt // SGU_CHUNK):
        rows = slice(n * SGU_CHUNK, (n + 1) * SGU_CHUNK)
        for gi in range(SGU_GROUPS):
            if gi % 2 == 0:
                ffn.step()
            csl = slice(gi * SGU_CH, (gi + 1) * SGU_CH)
            u = _gelu_tanh(p_scr[rows, 2 * HGRN_K + 2 * HGRN_V + gi * SGU_CH:
                                 2 * HGRN_K + 2 * HGRN_V + (gi + 1) * SGU_CH])
            v = _gelu_tanh(p_scr[rows, 2 * HGRN_K + 2 * HGRN_V + MIX_HALF + gi * SGU_CH:
                                 2 * HGRN_K + 2 * HGRN_V + MIX_HALF + (gi + 1) * SGU_CH])
            mu = jnp.mean(v, axis=-1, keepdims=True)
            vc = v - mu
            var = jnp.mean(vc * vc, axis=-1, keepdims=True)
            vn = vc * lax.rsqrt(var + EPS) * lng_ref[:, csl] + lnb_ref[:, csl]
            z = _dot(w_causal[gi], vn.astype(BF16)) + sb_ref[gi]
            mix_scr[rows, HGRN_V + gi * SGU_CH:HGRN_V + (gi + 1) * SGU_CH] = (u * z).astype(BF16)

    xm_scr[slot] = x + _dot(mix_scr[...], wout_ref[...])
    ffn.finish()


def _tile_index_maps(nt, steps):
    def mixer_tile(s):
        m = jnp.minimum(s, steps - 1)
        return m // nt, m % nt, 0

    def ffn_tile(s):
        f = jnp.maximum(s - 1, 0)
        return f // nt, f % nt, 0

    return mixer_tile, ffn_tile


def _even_layer(x, norm_w, w_in, w_out, lb, gnorm, ln_g, ln_b, sgu_w, sgu_b_bc,
                ffn_norm_w, w_gate, w_up, w_down):
    bsz, seq, _ = x.shape
    tt = min(MIX_TILE, seq)
    nt = seq // tt
    steps = bsz * nt
    mixer_tile, ffn_tile = _tile_index_maps(nt, steps)
    return pl.pallas_call(
        functools.partial(_even_kernel, tt=tt, nt=nt, steps=steps),
        grid=(steps + 1,),
        in_specs=[
            pl.BlockSpec((1, tt, D_MODEL), mixer_tile),
            _const_spec((1, D_MODEL)),
            _const_spec((D_MODEL, IN_EVEN)),
            _const_spec((D_MODEL, D_MODEL)),
            _const_spec((1, HGRN_K)),
            _const_spec((1, HGRN_V)),
            _const_spec((1, MIX_HALF)),
            _const_spec((1, MIX_HALF)),
            _const_spec((SGU_GROUPS, SGU_CHUNK, SGU_CHUNK)),
            _const_spec((SGU_GROUPS, SGU_CHUNK, SGU_CH)),
            _const_spec((1, D_MODEL)),
            _const_spec((D_MODEL, D_FF)),
            _const_spec((D_MODEL, D_FF)),
            _const_spec((D_FF, D_MODEL)),
        ],
        out_specs=pl.BlockSpec((1, tt, D_MODEL), ffn_tile),
        out_shape=jax.ShapeDtypeStruct(x.shape, F32),
        scratch_shapes=[
            pltpu.VMEM((tt, IN_EVEN), F32),
            pltpu.VMEM((HGRN_HEADS, HGRN_DV, HGRN_DK), F32),
            pltpu.VMEM((HGRN_HEADS, SUBLANES + HGRN_CHUNK, HGRN_DK), F32),
            pltpu.VMEM((HGRN_HEADS, SUBLANES + HGRN_CHUNK, HGRN_DK), F32),
            pltpu.VMEM((len(HGRN_LEVELS) + HGRN_DIAG, HGRN_CHUNK, HGRN_CHUNK), F32),
            pltpu.VMEM((tt, D_MODEL), BF16),
            pltpu.VMEM((2, tt, D_MODEL), F32),
            pltpu.VMEM((tt, D_FF), BF16),
        ],
        compiler_params=pltpu.CompilerParams(
            dimension_semantics=("arbitrary",), vmem_limit_bytes=VMEM_LIMIT_BYTES),
        name="even_layer",
    )(x, norm_w, w_in, w_out, lb, gnorm, ln_g, ln_b, sgu_w, sgu_b_bc,
      ffn_norm_w, w_gate, w_up, w_down)


ODD_CQ = 2 * CONV_CH
ODD_CKV = ODD_CQ + MLA_Q_RANK
ODD_ROPE = ODD_CKV + MLA_KV_RANK
IN_ODD_EXT = ODD_ROPE + 2 * MLA_ROPE
Q_SLOT = MLA_NOPE + 2 * MLA_ROPE


def _odd_kernel(x_ref, pos_ref, nw_ref, win_ref, wout_ref, cw_ref, cb_ref, clg_ref, clb_ref,
                qn_ref, wuq_ref, kvn_ref, wukv_ref, inv_ref, phase_ref, sgn_ref, o_ref,
                p_scr, hpad, cacc_scr, k_scr, vt_scr, q_scr, m_scr, l_scr, acc_scr, s_scr, mix_scr,
                *, tt):
    t_idx = pl.program_id(1)
    scale = (MLA_NOPE + MLA_ROPE) ** -0.5

    @pl.when(t_idx == 0)
    def _():
        hpad[:, 0:CONV_HALO, :] = jnp.zeros((CONV_CH // LANES, CONV_HALO, LANES), F32)

    x = x_ref[0]
    h = _rms(x, nw_ref[...]).astype(BF16)
    p_scr[...] = _dot(h, win_ref[...])

    for cb in range(CONV_CH // LANES):
        cs = slice(cb * LANES, (cb + 1) * LANES)
        a = p_scr[:, cs]
        gate = p_scr[:, CONV_CH + cb * LANES:CONV_CH + (cb + 1) * LANES]
        hpad[cb, CONV_HALO:CONV_HALO + tt, :] = a * _sigmoid(gate)
    for r in range(0, tt, CONV_ROWS):
        for cb in range(CONV_CH // LANES):
            cs = slice(cb * LANES, (cb + 1) * LANES)
            acc = jnp.broadcast_to(cb_ref[:, cs], (CONV_ROWS, LANES))
            for w in range(CONV_WIDTH):
                r0w = r + CONV_OFF + w
                acc = acc + hpad[cb, r0w:r0w + CONV_ROWS, :] * cw_ref[w:w + 1, cs]
            cacc_scr[r:r + CONV_ROWS, cs] = acc
    for cb in range(CONV_CH // LANES):
        hpad[cb, 0:CONV_HALO, :] = hpad[cb, tt:tt + CONV_HALO, :]
    acc = cacc_scr[...]
    mu = jnp.mean(acc, axis=-1, keepdims=True)
    ac = acc - mu
    var = jnp.mean(ac * ac, axis=-1, keepdims=True)
    cn = ac * lax.rsqrt(var + EPS) * clg_ref[...] + clb_ref[...]
    mix_scr[:, 0:CONV_CH] = (cn * _sigmoid(cn)).astype(BF16)

    cq = _rms(p_scr[:, ODD_CQ:ODD_CKV], qn_ref[...]).astype(BF16)
    ckv = _rms(p_scr[:, ODD_CKV:ODD_ROPE], kvn_ref[...]).astype(BF16)
    qf = _dot(cq, wuq_ref[...])
    kvf = _dot(ckv, wukv_ref[...])

    ang = pos_ref[0] * inv_ref[...] + phase_ref[...]
    rot = jnp.sin(ang) * sgn_ref[...]
    kr = p_scr[:, ODD_ROPE:IN_ODD_EXT] * rot
    kr = kr + pltpu.roll(kr, MLA_ROPE, axis=1)
    q_mult = jnp.concatenate([jnp.full((tt, MLA_NOPE), scale, F32), rot * scale], axis=1)

    r0 = pl.multiple_of(t_idx * tt, tt)
    for hd in range(MLA_HEADS):
        q_scr[hd] = (qf[:, hd * Q_SLOT:(hd + 1) * Q_SLOT] * q_mult).astype(BF16)
        kv0 = hd * (MLA_NOPE + MLA_V)
        k_scr[hd, pl.ds(r0, tt), 0:MLA_NOPE] = kvf[:, kv0:kv0 + MLA_NOPE].astype(BF16)
        k_scr[hd, pl.ds(r0, tt), MLA_NOPE:Q_SLOT] = kr.astype(BF16)
        vt_scr[hd, t_idx] = kvf[:, kv0 + MLA_NOPE:kv0 + MLA_NOPE + MLA_V].T.astype(BF16)
        m_scr[hd] = jnp.full((1, tt), NEG_BIG, F32)
        l_scr[hd] = jnp.zeros((1, tt), F32)
        acc_scr[hd] = jnp.zeros((MLA_V, tt), F32)

    heads = range(MLA_HEADS)

    def scores_into(kb, slot):
        k0 = pl.multiple_of(kb * tt, tt)
        for hd in heads:
            s_scr[slot, hd] = _dot_nt(k_scr[hd, pl.ds(k0, tt), :], q_scr[hd])

    def attend(kb, masked):
        sts = [s_scr[kb % 2, hd] for hd in heads]
        if not masked:
            scores_into(kb + 1, (kb + 1) % 2)
        if masked:
            krow = lax.broadcasted_iota(jnp.int32, (tt, tt), 0)
            qcol = lax.broadcasted_iota(jnp.int32, (tt, tt), 1)
            sts = [jnp.where(krow <= qcol, st, NEG_BIG) for st in sts]
        m_olds = [m_scr[hd] for hd in heads]
        m_news = [jnp.maximum(m_olds[hd], jnp.max(sts[hd], axis=0, keepdims=True)) for hd in heads]
        alphas = [jnp.exp(m_olds[hd] - m_news[hd]) for hd in heads]
        prs = [jnp.exp(sts[hd] - m_news[hd]) for hd in heads]
        for hd in heads:
            l_scr[hd] = alphas[hd] * l_scr[hd] + jnp.sum(prs[hd], axis=0, keepdims=True)
            m_scr[hd] = m_news[hd]
        pvs = [_dot(vt_scr[hd, kb], prs[hd].astype(BF16)) for hd in heads]
        for hd in heads:
            acc_scr[hd] = acc_scr[hd] * alphas[hd] + pvs[hd]

    def kv_body(kb, carry):
        attend(kb, False)
        return carry

    scores_into(0, 0)
    lax.fori_loop(0, t_idx, kv_body, 0)
    attend(t_idx, True)
    for hd in range(MLA_HEADS):
        o_t = acc_scr[hd] * (1.0 / l_scr[hd])
        mix_scr[:, CONV_CH + hd * MLA_V:CONV_CH + (hd + 1) * MLA_V] = o_t.T.astype(BF16)

    o_ref[0] = x + _dot(mix_scr[...], wout_ref[...])


def _odd_mixer(x, pos_bc, norm_w, w_in_ext, w_out, conv_w, conv_b, cln_g, cln_b,
               q_norm, w_uq_ext, kv_norm, w_ukv, inv_row, phase_row, sgn_row):
    bsz, seq, _ = x.shape
    tt = min(ODD_TILE, seq)
    return pl.pallas_call(
        functools.partial(_odd_kernel, tt=tt),
        grid=(bsz, seq // tt),
        in_specs=[
            pl.BlockSpec((1, tt, D_MODEL), lambda b, t: (b, t, 0)),
            pl.BlockSpec((1, tt, 2 * MLA_ROPE), lambda b, t: (b, t, 0)),
            _const_spec((1, D_MODEL)),
            _const_spec((D_MODEL, IN_ODD_EXT)),
            _const_spec((D_MODEL, D_MODEL)),
            _const_spec((CONV_WIDTH, CONV_CH)),
            _const_spec((1, CONV_CH)),
            _const_spec((1, CONV_CH)),
            _const_spec((1, CONV_CH)),
            _const_spec((1, MLA_Q_RANK)),
            _const_spec((MLA_Q_RANK, MLA_HEADS * Q_SLOT)),
            _const_spec((1, MLA_KV_RANK)),
            _const_spec((MLA_KV_RANK, MLA_HEADS * (MLA_NOPE + MLA_V))),
            _const_spec((1, 2 * MLA_ROPE)),
            _const_spec((1, 2 * MLA_ROPE)),
            _const_spec((1, 2 * MLA_ROPE)),
        ],
        out_specs=pl.BlockSpec((1, tt, D_MODEL), lambda b, t: (b, t, 0)),
        out_shape=jax.ShapeDtypeStruct(x.shape, F32),
        scratch_shapes=[
            pltpu.VMEM((tt, IN_ODD_EXT), F32),
            pltpu.VMEM((CONV_CH // LANES, CONV_HALO + tt, LANES), F32),
            pltpu.VMEM((tt, CONV_CH), F32),
            pltpu.VMEM((MLA_HEADS, seq, Q_SLOT), BF16),
            pltpu.VMEM((MLA_HEADS, seq // tt, MLA_V, tt), BF16),
            pltpu.VMEM((MLA_HEADS, tt, Q_SLOT), BF16),
            pltpu.VMEM((MLA_HEADS, 1, tt), F32),
            pltpu.VMEM((MLA_HEADS, 1, tt), F32),
            pltpu.VMEM((MLA_HEADS, MLA_V, tt), F32),
            pltpu.VMEM((2, MLA_HEADS, tt, tt), F32),
            pltpu.VMEM((tt, D_MODEL), BF16),
        ],
        compiler_params=pltpu.CompilerParams(
            dimension_semantics=("arbitrary", "arbitrary"), vmem_limit_bytes=VMEM_LIMIT_BYTES),
        name="odd_mixer",
    )(x, pos_bc, norm_w, w_in_ext, w_out, conv_w, conv_b, cln_g, cln_b,
      q_norm, w_uq_ext, kv_norm, w_ukv, inv_row, phase_row, sgn_row)


def _swap_halves(w):
    half = w.shape[-1] // 2
    return jnp.concatenate([w[..., half:], w[..., :half]], axis=-1)


def kernel(x, positions, mix_norm, ffn_norm, ffn_gate, ffn_up, ffn_down, w_in_even, w_out_even,
           hgrn_lb_logits, hgrn_gnorm, sgu_ln_g, sgu_ln_b, sgu_w, sgu_b, w_in_odd, w_out_odd,
           conv_w, conv_b, conv_ln_g, conv_ln_b, mla_q_norm, mla_w_uq, mla_kv_norm, mla_w_ukv,
           final_norm):
    bsz, seq, _ = x.shape
    depth = mix_norm.shape[0]
    lower_bounds = jnp.cumsum(jax.nn.softmax(hgrn_lb_logits.astype(F32), axis=0), axis=0)

    inv = 1.0 / (ROPE_THETA ** (jnp.arange(0, MLA_ROPE, 2, dtype=F32) / MLA_ROPE))
    inv_row = jnp.tile(inv, 4)[None, :]
    sgn_row = jnp.concatenate([jnp.ones((MLA_ROPE,), F32), -jnp.ones((MLA_ROPE // 2,), F32),
                               jnp.ones((MLA_ROPE // 2,), F32)])[None, :]
    phase_row = jnp.concatenate([jnp.full((MLA_ROPE,), math.pi / 2, F32),
                                 jnp.zeros((MLA_ROPE,), F32)])[None, :]
    pos_bc = jnp.broadcast_to(positions.astype(F32)[:, :, None], (bsz, seq, 2 * MLA_ROPE))

    row = lambda v: v.reshape(1, -1).astype(F32)
    for layer in range(depth):
        j = layer // 2
        if layer % 2 == 0:
            sgu_b_bc = jnp.broadcast_to(sgu_b[j][:, :, None], (SGU_GROUPS, SGU_CHUNK, SGU_CH))
            x = _even_layer(x, row(mix_norm[layer]), w_in_even[j].astype(BF16),
                            w_out_even[j].astype(BF16), row(lower_bounds[j]), row(hgrn_gnorm[j]),
                            row(sgu_ln_g[j]), row(sgu_ln_b[j]), sgu_w[j], sgu_b_bc,
                            row(ffn_norm[layer]), ffn_gate[layer].astype(BF16),
                            ffn_up[layer].astype(BF16), ffn_down[layer].astype(BF16))
            continue
        else:
            w_in = w_in_odd[j]
            w_in_ext = jnp.concatenate([w_in, _swap_halves(w_in[:, ODD_ROPE:])], axis=1).astype(BF16)
            wq = mla_w_uq[j].reshape(MLA_Q_RANK, MLA_HEADS, MLA_NOPE + MLA_ROPE)
            wq_ext = jnp.concatenate([wq, _swap_halves(wq[:, :, MLA_NOPE:])], axis=2)
            wq_ext = wq_ext.reshape(MLA_Q_RANK, MLA_HEADS * Q_SLOT).astype(BF16)
            x = _odd_mixer(x, pos_bc, row(mix_norm[layer]), w_in_ext, w_out_odd[j].astype(BF16),
                           conv_w[j], row(conv_b[j]), row(conv_ln_g[j]), row(conv_ln_b[j]),
                           row(mla_q_norm[j]), wq_ext, row(mla_kv_norm[j]),
                           mla_w_ukv[j].astype(BF16), inv_row, phase_row, sgn_row)
        x2 = _ffn(x.reshape(bsz * seq, D_MODEL), row(ffn_norm[layer]), ffn_gate[layer].astype(BF16),
                  ffn_up[layer].astype(BF16), ffn_down[layer].astype(BF16), row(final_norm),
                  final=(layer == depth - 1))
        x = x2.reshape(bsz, seq, D_MODEL)
    return x
```

```python
import functools
import math

import jax
import jax.numpy as jnp
from jax import lax
from jax.experimental import pallas as pl
from jax.experimental.pallas import tpu as pltpu

F32 = jnp.float32
BF16 = jnp.bfloat16

D_MODEL = 1024
MIX_HALF = D_MODEL // 2
HGRN_HEADS = 4
HGRN_DK = 128
HGRN_DV = MIX_HALF // HGRN_HEADS
HGRN_K = HGRN_HEADS * HGRN_DK
HGRN_V = HGRN_HEADS * HGRN_DV
SGU_GROUPS = 4
SGU_CH = MIX_HALF // SGU_GROUPS
SGU_CHUNK = 128
CONV_CH = MIX_HALF
CONV_WIDTH = 31
MLA_HEADS = 4
MLA_NOPE = 128
MLA_ROPE = 64
MLA_V = 128
MLA_Q_RANK = 384
MLA_KV_RANK = 256
ROPE_THETA = 10000.0
D_FF = -(-8 * D_MODEL // (3 * 256)) * 256
EPS = 1e-6
IN_EVEN = 2 * HGRN_K + 2 * HGRN_V + 2 * MIX_HALF
IN_ODD = 2 * CONV_CH + MLA_Q_RANK + MLA_KV_RANK + MLA_ROPE

LANES = 128
SUBLANES = 8
VMEM_LIMIT_BYTES = 56 * 1024 * 1024

MIX_TILE = 256
ODD_TILE = 512
FFN_TILE = 512
FFN_CHUNKS = ((0, 768), (768, 1536), (1536, 2304), (2304, D_FF))
FFN_UP_BLOCK = 512
FFN_DOWN_BLOCK = 512
HGRN_CHUNK = 64
HGRN_LEVELS = (32, 16, 8, 4)
HGRN_DIAG = 4
CONV_HALO = 32
CONV_OFF = CONV_HALO - (CONV_WIDTH - 1)
CONV_ROWS = 32
NEG_BIG = -1e30


def _rms(x, w):
    return x * lax.rsqrt(jnp.mean(x * x, axis=-1, keepdims=True) + EPS) * w


def _dot(a, b):
    return jnp.dot(a, b, preferred_element_type=F32)


def _dot_nt(a, b):
    return lax.dot_general(a, b, (((1,), (1,)), ((), ())), preferred_element_type=F32)


def _sigmoid(x):
    return 1.0 / (1.0 + jnp.exp(-x))


def _silu(x):
    hx = 0.5 * x
    return hx + hx * jnp.tanh(hx)


def _gelu_tanh(x):
    return 0.5 * x * (1.0 + jnp.tanh(math.sqrt(2.0 / math.pi) * (x + 0.044715 * (x * x * x))))


def _const_spec(shape):
    nd = len(shape)
    return pl.BlockSpec(shape, lambda *_: (0,) * nd, pipeline_mode=pl.Buffered(1))


class _TrailingFfn:
    def __init__(self, x_prev, nw_ref, wg_ref, wu_ref, wd_ref, act_scr, out_fn):
        self.x, self.wg_ref, self.wu_ref, self.wd_ref = x_prev, wg_ref, wu_ref, wd_ref
        self.act_scr, self.out_fn = act_scr, out_fn
        self.h = _rms(x_prev, nw_ref[...]).astype(BF16)
        up = [(self._up, c0) for c0 in range(0, D_FF, FFN_UP_BLOCK)]
        down = [(self._down, n0) for n0 in range(0, D_MODEL, FFN_DOWN_BLOCK)]
        self.todo = up + down

    def _up(self, c0):
        cols = slice(c0, min(c0 + FFN_UP_BLOCK, D_FF))
        g = _dot(self.h, self.wg_ref[:, cols])
        u = _dot(self.h, self.wu_ref[:, cols])
        self.act_scr[:, cols] = (_silu(g) * u).astype(BF16)

    def _down(self, n0):
        cols = slice(n0, n0 + FFN_DOWN_BLOCK)
        self.out_fn(cols, self.x[:, cols] + _dot(self.act_scr[...], self.wd_ref[:, cols]))

    def step(self, n=1):
        for _ in range(min(n, len(self.todo))):
            fn, arg = self.todo.pop(0)
            fn(arg)

    def finish(self):
        self.step(len(self.todo))


def _ffn_kernel(x_ref, nw_ref, wg_ref, wu_ref, wd_ref, fw_ref, o_ref, *, final):
    x = x_ref[...]
    h = _rms(x, nw_ref[...]).astype(BF16)
    acc = x
    for c0, c1 in FFN_CHUNKS:
        g = _dot(h, wg_ref[:, c0:c1])
        u = _dot(h, wu_ref[:, c0:c1])
        a = (_silu(g) * u).astype(BF16)
        acc = acc + _dot(a, wd_ref[c0:c1, :])
    if final:
        acc = _rms(acc, fw_ref[...])
    o_ref[...] = acc


def _ffn(x2, norm_w, w_gate, w_up, w_down, final_w, *, final):
    n = x2.shape[0]
    tm = min(FFN_TILE, n)
    return pl.pallas_call(
        functools.partial(_ffn_kernel, final=final),
        grid=(n // tm,),
        in_specs=[
            pl.BlockSpec((tm, D_MODEL), lambda i: (i, 0)),
            _const_spec((1, D_MODEL)),
            _const_spec((D_MODEL, D_FF)),
            _const_spec((D_MODEL, D_FF)),
            _const_spec((D_FF, D_MODEL)),
            _const_spec((1, D_MODEL)),
        ],
        out_specs=pl.BlockSpec((tm, D_MODEL), lambda i: (i, 0)),
        out_shape=jax.ShapeDtypeStruct((n, D_MODEL), F32),
        compiler_params=pltpu.CompilerParams(
            dimension_semantics=("arbitrary",), vmem_limit_bytes=VMEM_LIMIT_BYTES),
        name="ffn_final" if final else "ffn",
    )(x2, norm_w, w_gate, w_up, w_down, final_w)


def _even_kernel(x_ref, nw_ref, win_ref, wout_ref, lb_ref, gn_ref, lng_ref, lnb_ref,
                 sw_ref, sb_ref, fnw_ref, wg_ref, wu_ref, wd_ref, o_ref,
                 p_scr, st_scr, kpad, bpad, msk_scr, mix_scr, xm_scr, act_scr, *, tt, nt, steps):
    s_idx = pl.program_id(0)
    t_idx = lax.rem(jnp.minimum(s_idx, steps - 1), nt)
    slot = lax.rem(s_idx, 2)
    C, PAD = HGRN_CHUNK, SUBLANES
    heads = range(HGRN_HEADS)

    @pl.when(s_idx == 0)
    def _():
        xm_scr[1] = jnp.zeros((tt, D_MODEL), F32)

    @pl.when(t_idx == 0)
    def _():
        st_scr[...] = jnp.zeros_like(st_scr)
        kpad[:, 0:PAD, :] = jnp.zeros((HGRN_HEADS, PAD, HGRN_DK), F32)
        bpad[:, 0:PAD, :] = jnp.zeros((HGRN_HEADS, PAD, HGRN_DK), F32)

    def ffn_out(cols, val):
        o_ref[0, :, cols] = val

    ffn = _TrailingFfn(xm_scr[1 - slot], fnw_ref, wg_ref, wu_ref, wd_ref, act_scr, ffn_out)
    x = x_ref[0]
    h = _rms(x, nw_ref[...]).astype(BF16)
    p_scr[...] = _dot(h, win_ref[...])

    lb = lb_ref[...]
    ti = lax.broadcasted_iota(jnp.int32, (C, C), 0)
    si = lax.broadcasted_iota(jnp.int32, (C, C), 1)
    tril_c = jnp.where(ti >= si, 1.0, 0.0).astype(BF16)
    tril3 = jnp.concatenate([tril_c, tril_c, tril_c], axis=1)
    xr = jnp.bitwise_xor(ti, si)
    for li, hs in enumerate(HGRN_LEVELS):
        own = jnp.logical_and(lax.shift_right_logical(xr, hs.bit_length() - 1) == 1, ti > si)
        msk_scr[li] = jnp.where(own, 1.0, 0.0)
    for d in range(HGRN_DIAG):
        own = jnp.logical_and(ti - si == d, xr < HGRN_DIAG)
        msk_scr[len(HGRN_LEVELS) + d] = jnp.where(own, 1.0, 0.0)

    for c in range(tt // C):
        ffn.step()
        rows = slice(c * C, (c + 1) * C)
        q_all = p_scr[rows, 0:HGRN_K]
        f_pre = p_scr[rows, HGRN_K:2 * HGRN_K]
        iv_all = p_scr[rows, 2 * HGRN_K:2 * HGRN_K + HGRN_V]
        g_all = p_scr[rows, 2 * HGRN_K + HGRN_V:2 * HGRN_K + 2 * HGRN_V]
        f = lb + (1.0 - lb) * _sigmoid(f_pre)
        lf = jnp.log(f)
        lf_hi = lf.astype(BF16)
        lf_r = lf - lf_hi.astype(F32)
        lf_mid = lf_r.astype(BF16)
        lf_lo = (lf_r - lf_mid.astype(F32)).astype(BF16)
        b_all = _dot(tril3, jnp.concatenate([lf_hi, lf_mid, lf_lo], axis=0))
        k_all = 1.0 - f
        for hd in heads:
            kpad[hd, PAD:PAD + C, :] = k_all[:, hd * HGRN_DK:(hd + 1) * HGRN_DK]
            bpad[hd, PAD:PAD + C, :] = b_all[:, hd * HGRN_DK:(hd + 1) * HGRN_DK]

        def rows_of(ref, start, size):
            return jnp.concatenate([ref[hd, start:start + size, :] for hd in heads], axis=1)

        b_last = rows_of(bpad, PAD + C - 1, 1)
        q_inter = (q_all * jnp.exp(b_all)).astype(BF16)
        k_dec = (k_all * jnp.exp(b_last - b_all)).astype(BF16)
        dec_row = jnp.exp(b_last)
        iv_bf = iv_all.astype(BF16)
        gate_all = _silu(g_all)

        lev = []
        for hs in HGRN_LEVELS:
            pivots = [blk * 2 * hs + hs - 1 for blk in range(C // (2 * hs))]
            piv = jnp.concatenate(
                [jnp.broadcast_to(rows_of(bpad, PAD + p, 1), (2 * hs, HGRN_K)) for p in pivots],
                axis=0)
            e = jnp.exp(-jnp.abs(b_all - piv))
            lev.append(((q_all * e).astype(BF16), (k_all * e).astype(BF16)))
        diag = [q_all * k_all]
        for d in range(1, HGRN_DIAG):
            k_sh = rows_of(kpad, PAD - d, C)
            b_sh = rows_of(bpad, PAD - d, C)
            diag.append(q_all * k_sh * jnp.exp(b_all - b_sh))

        for hd in heads:
            if hd % 2 == 0:
                ffn.step()
            sl = slice(hd * HGRN_DK, (hd + 1) * HGRN_DK)
            vsl = slice(hd * HGRN_DV, (hd + 1) * HGRN_DV)
            sc = jnp.zeros((C, C), F32)
            for li in range(len(HGRN_LEVELS)):
                sc = sc + _dot_nt(lev[li][0][:, sl], lev[li][1][:, sl]) * msk_scr[li]
            for d in range(HGRN_DIAG):
                sc = sc + (jnp.sum(diag[d][:, sl], axis=-1, keepdims=True)
                           * msk_scr[len(HGRN_LEVELS) + d])
            st = st_scr[hd]
            o = _dot_nt(q_inter[:, sl], st.astype(BF16)) + _dot(sc.astype(BF16), iv_bf[:, vsl])
            st_scr[hd] = st * dec_row[:, sl] + _dot(iv_all[:, vsl].T.astype(BF16), k_dec[:, sl])
            on = _rms(o, gn_ref[:, vsl])
            mix_scr[rows, vsl] = (on * gate_all[:, vsl]).astype(BF16)

    prow = lax.broadcasted_iota(jnp.int32, (SGU_CHUNK, SGU_CHUNK), 0)
    pcol = lax.broadcasted_iota(jnp.int32, (SGU_CHUNK, SGU_CHUNK), 1)
    w_causal = [jnp.where(prow >= pcol, sw_ref[gi], 0.0).astype(BF16) for gi in range(SGU_GROUPS)]
    for n in range(tt // SGU_CHUNK):
        rows = slice(n * SGU_CHUNK, (n + 1) * SGU_CHUNK)
        for gi in range(SGU_GROUPS):
            if gi % 2 == 0:
                ffn.step()
            csl = slice(gi * SGU_CH, (gi + 1) * SGU_CH)
            u = _gelu_tanh(p_scr[rows, 2 * HGRN_K + 2 * HGRN_V + gi * SGU_CH:
                                 2 * HGRN_K + 2 * HGRN_V + (gi + 1) * SGU_CH])
            v = _gelu_tanh(p_scr[rows, 2 * HGRN_K + 2 * HGRN_V + MIX_HALF + gi * SGU_CH:
                                 2 * HGRN_K + 2 * HGRN_V + MIX_HALF + (gi + 1) * SGU_CH])
            mu = jnp.mean(v, axis=-1, keepdims=True)
            vc = v - mu
            var = jnp.mean(vc * vc, axis=-1, keepdims=True)
            vn = vc * lax.rsqrt(var + EPS) * lng_ref[:, csl] + lnb_ref[:, csl]
            z = _dot(w_causal[gi], vn.astype(BF16)) + sb_ref[gi]
            mix_scr[rows, HGRN_V + gi * SGU_CH:HGRN_V + (gi + 1) * SGU_CH] = (u * z).astype(BF16)

    xm_scr[slot] = x + _dot(mix_scr[...], wout_ref[...])
    ffn.finish()


def _tile_index_maps(nt, steps):
    def mixer_tile(s):
        m = jnp.minimum(s, steps - 1)
        return m // nt, m % nt, 0

    def ffn_tile(s):
        f = jnp.maximum(s - 1, 0)
        return f // nt, f % nt, 0

    return mixer_tile, ffn_tile


def _even_layer(x, norm_w, w_in, w_out, lb, gnorm, ln_g, ln_b, sgu_w, sgu_b_bc,
                ffn_norm_w, w_gate, w_up, w_down):
    bsz, seq, _ = x.shape
    tt = min(MIX_TILE, seq)
    nt = seq // tt
    steps = bsz * nt
    mixer_tile, ffn_tile = _tile_index_maps(nt, steps)
    return pl.pallas_call(
        functools.partial(_even_kernel, tt=tt, nt=nt, steps=steps),
        grid=(steps + 1,),
        in_specs=[
            pl.BlockSpec((1, tt, D_MODEL), mixer_tile),
            _const_spec((1, D_MODEL)),
            _const_spec((D_MODEL, IN_EVEN)),
            _const_spec((D_MODEL, D_MODEL)),
            _const_spec((1, HGRN_K)),
            _const_spec((1, HGRN_V)),
            _const_spec((1, MIX_HALF)),
            _const_spec((1, MIX_HALF)),
            _const_spec((SGU_GROUPS, SGU_CHUNK, SGU_CHUNK)),
            _const_spec((SGU_GROUPS, SGU_CHUNK, SGU_CH)),
            _const_spec((1, D_MODEL)),
            _const_spec((D_MODEL, D_FF)),
            _const_spec((D_MODEL, D_FF)),
            _const_spec((D_FF, D_MODEL)),
        ],
        out_specs=pl.BlockSpec((1, tt, D_MODEL), ffn_tile),
        out_shape=jax.ShapeDtypeStruct(x.shape, F32),
        scratch_shapes=[
            pltpu.VMEM((tt, IN_EVEN), F32),
            pltpu.VMEM((HGRN_HEADS, HGRN_DV, HGRN_DK), F32),
            pltpu.VMEM((HGRN_HEADS, SUBLANES + HGRN_CHUNK, HGRN_DK), F32),
            pltpu.VMEM((HGRN_HEADS, SUBLANES + HGRN_CHUNK, HGRN_DK), F32),
            pltpu.VMEM((len(HGRN_LEVELS) + HGRN_DIAG, HGRN_CHUNK, HGRN_CHUNK), F32),
            pltpu.VMEM((tt, D_MODEL), BF16),
            pltpu.VMEM((2, tt, D_MODEL), F32),
            pltpu.VMEM((tt, D_FF), BF16),
        ],
        compiler_params=pltpu.CompilerParams(
            dimension_semantics=("arbitrary",), vmem_limit_bytes=VMEM_LIMIT_BYTES),
        name="even_layer",
    )(x, norm_w, w_in, w_out, lb, gnorm, ln_g, ln_b, sgu_w, sgu_b_bc,
      ffn_norm_w, w_gate, w_up, w_down)


ODD_CQ = 2 * CONV_CH
ODD_CKV = ODD_CQ + MLA_Q_RANK
ODD_ROPE = ODD_CKV + MLA_KV_RANK
IN_ODD_EXT = ODD_ROPE + 2 * MLA_ROPE
Q_SLOT = MLA_NOPE + 2 * MLA_ROPE


def _odd_kernel(x_ref, pos_ref, nw_ref, win_ref, wout_ref, cw_ref, cb_ref, clg_ref, clb_ref,
                qn_ref, wuq_ref, kvn_ref, wukv_ref, inv_ref, phase_ref, sgn_ref, o_ref,
                p_scr, hpad, cacc_scr, k_scr, vt_scr, q_scr, m_scr, l_scr, acc_scr, s_scr, mix_scr,
                *, tt):
    t_idx = pl.program_id(1)
    scale = (MLA_NOPE + MLA_ROPE) ** -0.5

    @pl.when(t_idx == 0)
    def _():
        hpad[:, 0:CONV_HALO, :] = jnp.zeros((CONV_CH // LANES, CONV_HALO, LANES), F32)

    x = x_ref[0]
    h = _rms(x, nw_ref[...]).astype(BF16)
    p_scr[...] = _dot(h, win_ref[...])

    for cb in range(CONV_CH // LANES):
        cs = slice(cb * LANES, (cb + 1) * LANES)
        a = p_scr[:, cs]
        gate = p_scr[:, CONV_CH + cb * LANES:CONV_CH + (cb + 1) * LANES]
        hpad[cb, CONV_HALO:CONV_HALO + tt, :] = a * _sigmoid(gate)
    for r in range(0, tt, CONV_ROWS):
        for cb in range(CONV_CH // LANES):
            cs = slice(cb * LANES, (cb + 1) * LANES)
            acc = jnp.broadcast_to(cb_ref[:, cs], (CONV_ROWS, LANES))
            for w in range(CONV_WIDTH):
                r0w = r + CONV_OFF + w
                acc = acc + hpad[cb, r0w:r0w + CONV_ROWS, :] * cw_ref[w:w + 1, cs]
            cacc_scr[r:r + CONV_ROWS, cs] = acc
    for cb in range(CONV_CH // LANES):
        hpad[cb, 0:CONV_HALO, :] = hpad[cb, tt:tt + CONV_HALO, :]
    acc = cacc_scr[...]
    mu = jnp.mean(acc, axis=-1, keepdims=True)
    ac = acc - mu
    var = jnp.mean(ac * ac, axis=-1, keepdims=True)
    cn = ac * lax.rsqrt(var + EPS) * clg_ref[...] + clb_ref[...]
    mix_scr[:, 0:CONV_CH] = _silu(cn).astype(BF16)

    cq = _rms(p_scr[:, ODD_CQ:ODD_CKV], qn_ref[...]).astype(BF16)
    ckv = _rms(p_scr[:, ODD_CKV:ODD_ROPE], kvn_ref[...]).astype(BF16)
    qf = _dot(cq, wuq_ref[...])
    kvf = _dot(ckv, wukv_ref[...])

    ang = pos_ref[0] * inv_ref[...] + phase_ref[...]
    rot = jnp.sin(ang) * sgn_ref[...]
    kr = p_scr[:, ODD_ROPE:IN_ODD_EXT] * rot
    kr = kr + pltpu.roll(kr, MLA_ROPE, axis=1)
    q_mult = jnp.concatenate([jnp.full((tt, MLA_NOPE), scale, F32), rot * scale], axis=1)

    r0 = pl.multiple_of(t_idx * tt, tt)
    for hd in range(MLA_HEADS):
        q_scr[hd] = (qf[:, hd * Q_SLOT:(hd + 1) * Q_SLOT] * q_mult).astype(BF16)
        kv0 = hd * (MLA_NOPE + MLA_V)
        k_scr[hd, pl.ds(r0, tt), 0:MLA_NOPE] = kvf[:, kv0:kv0 + MLA_NOPE].astype(BF16)
        k_scr[hd, pl.ds(r0, tt), MLA_NOPE:Q_SLOT] = kr.astype(BF16)
        vt_scr[hd, t_idx] = kvf[:, kv0 + MLA_NOPE:kv0 + MLA_NOPE + MLA_V].T.astype(BF16)
        m_scr[hd] = jnp.full((1, tt), NEG_BIG, F32)
        l_scr[hd] = jnp.zeros((1, tt), F32)
        acc_scr[hd] = jnp.zeros((MLA_V, tt), F32)

    heads = range(MLA_HEADS)

    def scores_into(kb, slot):
        k0 = pl.multiple_of(kb * tt, tt)
        for hd in heads:
            s_scr[slot, hd] = _dot_nt(k_scr[hd, pl.ds(k0, tt), :], q_scr[hd])

    def attend(kb, masked):
        sts = [s_scr[kb % 2, hd] for hd in heads]
        if not masked:
            scores_into(kb + 1, (kb + 1) % 2)
        if masked:
            krow = lax.broadcasted_iota(jnp.int32, (tt, tt), 0)
            qcol = lax.broadcasted_iota(jnp.int32, (tt, tt), 1)
            sts = [jnp.where(krow <= qcol, st, NEG_BIG) for st in sts]
        m_olds = [m_scr[hd] for hd in heads]
        m_news = [jnp.maximum(m_olds[hd], jnp.max(sts[hd], axis=0, keepdims=True)) for hd in heads]
        alphas = [jnp.exp(m_olds[hd] - m_news[hd]) for hd in heads]
        prs = [jnp.exp(sts[hd] - m_news[hd]) for hd in heads]
        for hd in heads:
            l_scr[hd] = alphas[hd] * l_scr[hd] + jnp.sum(prs[hd], axis=0, keepdims=True)
            m_scr[hd] = m_news[hd]
        pvs = [_dot(vt_scr[hd, kb], prs[hd].astype(BF16)) for hd in heads]
        for hd in heads:
            acc_scr[hd] = acc_scr[hd] * alphas[hd] + pvs[hd]

    def kv_body(kb, carry):
        attend(kb, False)
        return carry

    scores_into(0, 0)
    lax.fori_loop(0, t_idx, kv_body, 0)
    attend(t_idx, True)
    for hd in range(MLA_HEADS):
        o_t = acc_scr[hd] * (1.0 / l_scr[hd])
        mix_scr[:, CONV_CH + hd * MLA_V:CONV_CH + (hd + 1) * MLA_V] = o_t.T.astype(BF16)

    o_ref[0] = x + _dot(mix_scr[...], wout_ref[...])


def _odd_mixer(x, pos_bc, norm_w, w_in_ext, w_out, conv_w, conv_b, cln_g, cln_b,
               q_norm, w_uq_ext, kv_norm, w_ukv, inv_row, phase_row, sgn_row):
    bsz, seq, _ = x.shape
    tt = min(ODD_TILE, seq)
    return pl.pallas_call(
        functools.partial(_odd_kernel, tt=tt),
        grid=(bsz, seq // tt),
        in_specs=[
            pl.BlockSpec((1, tt, D_MODEL), lambda b, t: (b, t, 0)),
            pl.BlockSpec((1, tt, 2 * MLA_ROPE), lambda b, t: (b, t, 0)),
            _const_spec((1, D_MODEL)),
            _const_spec((D_MODEL, IN_ODD_EXT)),
            _const_spec((D_MODEL, D_MODEL)),
            _const_spec((CONV_WIDTH, CONV_CH)),
            _const_spec((1, CONV_CH)),
            _const_spec((1, CONV_CH)),
            _const_spec((1, CONV_CH)),
            _const_spec((1, MLA_Q_RANK)),
            _const_spec((MLA_Q_RANK, MLA_HEADS * Q_SLOT)),
            _const_spec((1, MLA_KV_RANK)),
            _const_spec((MLA_KV_RANK, MLA_HEADS * (MLA_NOPE + MLA_V))),
            _const_spec((1, 2 * MLA_ROPE)),
            _const_spec((1, 2 * MLA_ROPE)),
            _const_spec((1, 2 * MLA_ROPE)),
        ],
        out_specs=pl.BlockSpec((1, tt, D_MODEL), lambda b, t: (b, t, 0)),
        out_shape=jax.ShapeDtypeStruct(x.shape, F32),
        scratch_shapes=[
            pltpu.VMEM((tt, IN_ODD_EXT), F32),
            pltpu.VMEM((CONV_CH // LANES, CONV_HALO + tt, LANES), F32),
            pltpu.VMEM((tt, CONV_CH), F32),
            pltpu.VMEM((MLA_HEADS, seq, Q_SLOT), BF16),
            pltpu.VMEM((MLA_HEADS, seq // tt, MLA_V, tt), BF16),
            pltpu.VMEM((MLA_HEADS, tt, Q_SLOT), BF16),
            pltpu.VMEM((MLA_HEADS, 1, tt), F32),
            pltpu.VMEM((MLA_HEADS, 1, tt), F32),
            pltpu.VMEM((MLA_HEADS, MLA_V, tt), F32),
            pltpu.VMEM((2, MLA_HEADS, tt, tt), F32),
            pltpu.VMEM((tt, D_MODEL), BF16),
        ],
        compiler_params=pltpu.CompilerParams(
            dimension_semantics=("arbitrary", "arbitrary"), vmem_limit_bytes=VMEM_LIMIT_BYTES),
        name="odd_mixer",
    )(x, pos_bc, norm_w, w_in_ext, w_out, conv_w, conv_b, cln_g, cln_b,
      q_norm, w_uq_ext, kv_norm, w_ukv, inv_row, phase_row, sgn_row)


def _swap_halves(w):
    half = w.shape[-1] // 2
    return jnp.concatenate([w[..., half:], w[..., :half]], axis=-1)


def kernel(x, positions, mix_norm, ffn_norm, ffn_gate, ffn_up, ffn_down, w_in_even, w_out_even,
           hgrn_lb_logits, hgrn_gnorm, sgu_ln_g, sgu_ln_b, sgu_w, sgu_b, w_in_odd, w_out_odd,
           conv_w, conv_b, conv_ln_g, conv_ln_b, mla_q_norm, mla_w_uq, mla_kv_norm, mla_w_ukv,
           final_norm):
    bsz, seq, _ = x.shape
    depth = mix_norm.shape[0]
    lower_bounds = jnp.cumsum(jax.nn.softmax(hgrn_lb_logits.astype(F32), axis=0), axis=0)

    inv = 1.0 / (ROPE_THETA ** (jnp.arange(0, MLA_ROPE, 2, dtype=F32) / MLA_ROPE))
    inv_row = jnp.tile(inv, 4)[None, :]
    sgn_row = jnp.concatenate([jnp.ones((MLA_ROPE,), F32), -jnp.ones((MLA_ROPE // 2,), F32),
                               jnp.ones((MLA_ROPE // 2,), F32)])[None, :]
    phase_row = jnp.concatenate([jnp.full((MLA_ROPE,), math.pi / 2, F32),
                                 jnp.zeros((MLA_ROPE,), F32)])[None, :]
    pos_bc = jnp.broadcast_to(positions.astype(F32)[:, :, None], (bsz, seq, 2 * MLA_ROPE))

    row = lambda v: v.reshape(1, -1).astype(F32)
    for layer in range(depth):
        j = layer // 2
        if layer % 2 == 0:
            sgu_b_bc = jnp.broadcast_to(sgu_b[j][:, :, None], (SGU_GROUPS, SGU_CHUNK, SGU_CH))
            x = _even_layer(x, row(mix_norm[layer]), w_in_even[j].astype(BF16),
                            w_out_even[j].astype(BF16), row(lower_bounds[j]), row(hgrn_gnorm[j]),
                            row(sgu_ln_g[j]), row(sgu_ln_b[j]), sgu_w[j], sgu_b_bc,
                            row(ffn_norm[layer]), ffn_gate[layer].astype(BF16),
                            ffn_up[layer].astype(BF16), ffn_down[layer].astype(BF16))
            continue
        else:
            w_in = w_in_odd[j]
            w_in_ext = jnp.concatenate([w_in, _swap_halves(w_in[:, ODD_ROPE:])], axis=1).astype(BF16)
            wq = mla_w_uq[j].reshape(MLA_Q_RANK, MLA_HEADS, MLA_NOPE + MLA_ROPE)
            wq_ext = jnp.concatenate([wq, _swap_halves(wq[:, :, MLA_NOPE:])], axis=2)
            wq_ext = wq_ext.reshape(MLA_Q_RANK, MLA_HEADS * Q_SLOT).astype(BF16)
            x = _odd_mixer(x, pos_bc, row(mix_norm[layer]), w_in_ext, w_out_odd[j].astype(BF16),
                           conv_w[j], row(conv_b[j]), row(conv_ln_g[j]), row(conv_ln_b[j]),
                           row(mla_q_norm[j]), wq_ext, row(mla_kv_norm[j]),
                           mla_w_ukv[j].astype(BF16), inv_row, phase_row, sgn_row)
        x2 = _ffn(x.reshape(bsz * seq, D_MODEL), row(ffn_norm[layer]), ffn_gate[layer].astype(BF16),
                  ffn_up[layer].astype(BF16), ffn_down[layer].astype(BF16), row(final_norm),
                  final=(layer == depth - 1))
        x = x2.reshape(bsz, seq, D_MODEL)
    return x
```

```python
import functools
import math

import jax
import jax.numpy as jnp
from jax import lax
from jax.experimental import pallas as pl
from jax.experimental.pallas import tpu as pltpu

F32 = jnp.float32
BF16 = jnp.bfloat16

D_MODEL = 1024
MIX_HALF = D_MODEL // 2
HGRN_HEADS = 4
HGRN_DK = 128
HGRN_DV = MIX_HALF // HGRN_HEADS
HGRN_K = HGRN_HEADS * HGRN_DK
HGRN_V = HGRN_HEADS * HGRN_DV
SGU_GROUPS = 4
SGU_CH = MIX_HALF // SGU_GROUPS
SGU_CHUNK = 128
CONV_CH = MIX_HALF
CONV_WIDTH = 31
MLA_HEADS = 4
MLA_NOPE = 128
MLA_ROPE = 64
MLA_V = 128
MLA_Q_RANK = 384
MLA_KV_RANK = 256
ROPE_THETA = 10000.0
D_FF = -(-8 * D_MODEL // (3 * 256)) * 256
EPS = 1e-6
IN_EVEN = 2 * HGRN_K + 2 * HGRN_V + 2 * MIX_HALF
IN_ODD = 2 * CONV_CH + MLA_Q_RANK + MLA_KV_RANK + MLA_ROPE

LANES = 128
SUBLANES = 8
VMEM_LIMIT_BYTES = 56 * 1024 * 1024

MIX_TILE = 256
ODD_TILE = 512
FFN_TILE = 512
FFN_CHUNKS = ((0, 768), (768, 1536), (1536, 2304), (2304, D_FF))
FFN_UP_BLOCK = 512
FFN_DOWN_BLOCK = 512
HGRN_CHUNK = 64
HGRN_LEVELS = (32, 16, 8, 4)
HGRN_DIAG = 4
CONV_HALO = 32
CONV_OFF = CONV_HALO - (CONV_WIDTH - 1)
CONV_ROWS = 32
NEG_BIG = -1e30


def _rms(x, w):
    return x * lax.rsqrt(jnp.mean(x * x, axis=-1, keepdims=True) + EPS) * w


def _dot(a, b):
    return jnp.dot(a, b, preferred_element_type=F32)


def _dot_nt(a, b):
    return lax.dot_general(a, b, (((1,), (1,)), ((), ())), preferred_element_type=F32)


def _sigmoid(x):
    return 1.0 / (1.0 + jnp.exp(-x))


def _silu(x):
    hx = 0.5 * x
    return hx + hx * jnp.tanh(hx)


def _gelu_tanh(x):
    c = math.sqrt(2.0 / math.pi)
    hx = 0.5 * x
    return hx + hx * jnp.tanh(x * (c + (c * 0.044715) * (x * x)))


def _const_spec(shape):
    nd = len(shape)
    return pl.BlockSpec(shape, lambda *_: (0,) * nd, pipeline_mode=pl.Buffered(1))


class _TrailingFfn:
    def __init__(self, x_prev, nw_ref, wg_ref, wu_ref, wd_ref, act_scr, out_fn):
        self.x, self.wg_ref, self.wu_ref, self.wd_ref = x_prev, wg_ref, wu_ref, wd_ref
        self.act_scr, self.out_fn = act_scr, out_fn
        self.h = _rms(x_prev, nw_ref[...]).astype(BF16)
        up = [(self._up, c0) for c0 in range(0, D_FF, FFN_UP_BLOCK)]
        down = [(self._down, n0) for n0 in range(0, D_MODEL, FFN_DOWN_BLOCK)]
        self.todo = up + down

    def _up(self, c0):
        cols = slice(c0, min(c0 + FFN_UP_BLOCK, D_FF))
        g = _dot(self.h, self.wg_ref[:, cols])
        u = _dot(self.h, self.wu_ref[:, cols])
        self.act_scr[:, cols] = (_silu(g) * u).astype(BF16)

    def _down(self, n0):
        cols = slice(n0, n0 + FFN_DOWN_BLOCK)
        self.out_fn(cols, self.x[:, cols] + _dot(self.act_scr[...], self.wd_ref[:, cols]))

    def step(self, n=1):
        for _ in range(min(n, len(self.todo))):
            fn, arg = self.todo.pop(0)
            fn(arg)

    def finish(self):
        self.step(len(self.todo))


def _ffn_kernel(x_ref, nw_ref, wg_ref, wu_ref, wd_ref, fw_ref, o_ref, *, final):
    x = x_ref[...]
    h = _rms(x, nw_ref[...]).astype(BF16)
    acc = x
    for c0, c1 in FFN_CHUNKS:
        g = _dot(h, wg_ref[:, c0:c1])
        u = _dot(h, wu_ref[:, c0:c1])
        a = (_silu(g) * u).astype(BF16)
        acc = acc + _dot(a, wd_ref[c0:c1, :])
    if final:
        acc = _rms(acc, fw_ref[...])
    o_ref[...] = acc


def _ffn(x2, norm_w, w_gate, w_up, w_down, final_w, *, final):
    n = x2.shape[0]
    tm = min(FFN_TILE, n)
    return pl.pallas_call(
        functools.partial(_ffn_kernel, final=final),
        grid=(n // tm,),
        in_specs=[
            pl.BlockSpec((tm, D_MODEL), lambda i: (i, 0)),
            _const_spec((1, D_MODEL)),
            _const_spec((D_MODEL, D_FF)),
            _const_spec((D_MODEL, D_FF)),
            _const_spec((D_FF, D_MODEL)),
            _const_spec((1, D_MODEL)),
        ],
        out_specs=pl.BlockSpec((tm, D_MODEL), lambda i: (i, 0)),
        out_shape=jax.ShapeDtypeStruct((n, D_MODEL), F32),
        compiler_params=pltpu.CompilerParams(
            dimension_semantics=("arbitrary",), vmem_limit_bytes=VMEM_LIMIT_BYTES),
        name="ffn_final" if final else "ffn",
    )(x2, norm_w, w_gate, w_up, w_down, final_w)


def _even_kernel(x_ref, nw_ref, win_ref, wout_ref, lb_ref, gn_ref, lng_ref, lnb_ref,
                 sw_ref, sb_ref, fnw_ref, wg_ref, wu_ref, wd_ref, o_ref,
                 p_scr, st_scr, kpad, bpad, msk_scr, mix_scr, xm_scr, act_scr, *, tt, nt, steps):
    s_idx = pl.program_id(0)
    t_idx = lax.rem(jnp.minimum(s_idx, steps - 1), nt)
    slot = lax.rem(s_idx, 2)
    C, PAD = HGRN_CHUNK, SUBLANES
    heads = range(HGRN_HEADS)

    @pl.when(s_idx == 0)
    def _():
        xm_scr[1] = jnp.zeros((tt, D_MODEL), F32)

    @pl.when(t_idx == 0)
    def _():
        st_scr[...] = jnp.zeros_like(st_scr)
        kpad[:, 0:PAD, :] = jnp.zeros((HGRN_HEADS, PAD, HGRN_DK), F32)
        bpad[:, 0:PAD, :] = jnp.zeros((HGRN_HEADS, PAD, HGRN_DK), F32)

    def ffn_out(cols, val):
        o_ref[0, :, cols] = val

    ffn = _TrailingFfn(xm_scr[1 - slot], fnw_ref, wg_ref, wu_ref, wd_ref, act_scr, ffn_out)
    x = x_ref[0]
    h = _rms(x, nw_ref[...]).astype(BF16)
    p_scr[...] = _dot(h, win_ref[...])

    lb = lb_ref[...]
    ti = lax.broadcasted_iota(jnp.int32, (C, C), 0)
    si = lax.broadcasted_iota(jnp.int32, (C, C), 1)
    tril_c = jnp.where(ti >= si, 1.0, 0.0).astype(BF16)
    tril3 = jnp.concatenate([tril_c, tril_c, tril_c], axis=1)
    xr = jnp.bitwise_xor(ti, si)
    for li, hs in enumerate(HGRN_LEVELS):
        own = jnp.logical_and(lax.shift_right_logical(xr, hs.bit_length() - 1) == 1, ti > si)
        msk_scr[li] = jnp.where(own, 1.0, 0.0)
    for d in range(HGRN_DIAG):
        own = jnp.logical_and(ti - si == d, xr < HGRN_DIAG)
        msk_scr[len(HGRN_LEVELS) + d] = jnp.where(own, 1.0, 0.0)

    for c in range(tt // C):
        ffn.step()
        rows = slice(c * C, (c + 1) * C)
        q_all = p_scr[rows, 0:HGRN_K]
        f_pre = p_scr[rows, HGRN_K:2 * HGRN_K]
        iv_all = p_scr[rows, 2 * HGRN_K:2 * HGRN_K + HGRN_V]
        g_all = p_scr[rows, 2 * HGRN_K + HGRN_V:2 * HGRN_K + 2 * HGRN_V]
        f = lb + (1.0 - lb) * _sigmoid(f_pre)
        lf = jnp.log(f)
        lf_hi = lf.astype(BF16)
        lf_r = lf - lf_hi.astype(F32)
        lf_mid = lf_r.astype(BF16)
        lf_lo = (lf_r - lf_mid.astype(F32)).astype(BF16)
        b_all = _dot(tril3, jnp.concatenate([lf_hi, lf_mid, lf_lo], axis=0))
        k_all = 1.0 - f
        for hd in heads:
            kpad[hd, PAD:PAD + C, :] = k_all[:, hd * HGRN_DK:(hd + 1) * HGRN_DK]
            bpad[hd, PAD:PAD + C, :] = b_all[:, hd * HGRN_DK:(hd + 1) * HGRN_DK]

        def rows_of(ref, start, size):
            return jnp.concatenate([ref[hd, start:start + size, :] for hd in heads], axis=1)

        b_last = rows_of(bpad, PAD + C - 1, 1)
        q_inter = (q_all * jnp.exp(b_all)).astype(BF16)
        k_dec = (k_all * jnp.exp(b_last - b_all)).astype(BF16)
        dec_row = jnp.exp(b_last)
        iv_bf = iv_all.astype(BF16)
        gate_all = _silu(g_all)

        lev = []
        for hs in HGRN_LEVELS:
            pivots = [blk * 2 * hs + hs - 1 for blk in range(C // (2 * hs))]
            piv = jnp.concatenate(
                [jnp.broadcast_to(rows_of(bpad, PAD + p, 1), (2 * hs, HGRN_K)) for p in pivots],
                axis=0)
            e = jnp.exp(-jnp.abs(b_all - piv))
            lev.append(((q_all * e).astype(BF16), (k_all * e).astype(BF16)))
        diag = [q_all * k_all]
        for d in range(1, HGRN_DIAG):
            k_sh = rows_of(kpad, PAD - d, C)
            b_sh = rows_of(bpad, PAD - d, C)
            diag.append(q_all * k_sh * jnp.exp(b_all - b_sh))

        for hd in heads:
            if hd % 2 == 0:
                ffn.step()
            sl = slice(hd * HGRN_DK, (hd + 1) * HGRN_DK)
            vsl = slice(hd * HGRN_DV, (hd + 1) * HGRN_DV)
            sc = jnp.zeros((C, C), F32)
            for li in range(len(HGRN_LEVELS)):
                sc = sc + _dot_nt(lev[li][0][:, sl], lev[li][1][:, sl]) * msk_scr[li]
            for d in range(HGRN_DIAG):
                sc = sc + (jnp.sum(diag[d][:, sl], axis=-1, keepdims=True)
                           * msk_scr[len(HGRN_LEVELS) + d])
            st = st_scr[hd]
            o = _dot_nt(q_inter[:, sl], st.astype(BF16)) + _dot(sc.astype(BF16), iv_bf[:, vsl])
            st_scr[hd] = st * dec_row[:, sl] + _dot(iv_all[:, vsl].T.astype(BF16), k_dec[:, sl])
            on = _rms(o, gn_ref[:, vsl])
            mix_scr[rows, vsl] = (on * gate_all[:, vsl]).astype(BF16)

    prow = lax.broadcasted_iota(jnp.int32, (SGU_CHUNK, SGU_CHUNK), 0)
    pcol = lax.broadcasted_iota(jnp.int32, (SGU_CHUNK, SGU_CHUNK), 1)
    w_causal = [jnp.where(prow >= pcol, sw_ref[gi], 0.0).astype(BF16) for gi in range(SGU_GROUPS)]
    for n in range(tt // SGU_CHUNK):
        rows = slice(n * SGU_CHUNK, (n + 1) * SGU_CHUNK)
        for gi in range(SGU_GROUPS):
            if gi % 2 == 0:
                ffn.step()
            csl = slice(gi * SGU_CH, (gi + 1) * SGU_CH)
            u = _gelu_tanh(p_scr[rows, 2 * HGRN_K + 2 * HGRN_V + gi * SGU_CH:
                                 2 * HGRN_K + 2 * HGRN_V + (gi + 1) * SGU_CH])
            v = _gelu_tanh(p_scr[rows, 2 * HGRN_K + 2 * HGRN_V + MIX_HALF + gi * SGU_CH:
                                 2 * HGRN_K + 2 * HGRN_V + MIX_HALF + (gi + 1) * SGU_CH])
            mu = jnp.mean(v, axis=-1, keepdims=True)
            vc = v - mu
            var = jnp.mean(vc * vc, axis=-1, keepdims=True)
            vn = vc * lax.rsqrt(var + EPS) * lng_ref[:, csl] + lnb_ref[:, csl]
            z = _dot(w_causal[gi], vn.astype(BF16)) + sb_ref[gi]
            mix_scr[rows, HGRN_V + gi * SGU_CH:HGRN_V + (gi + 1) * SGU_CH] = (u * z).astype(BF16)

    xm_scr[slot] = x + _dot(mix_scr[...], wout_ref[...])
    ffn.finish()


def _tile_index_maps(nt, steps):
    def mixer_tile(s):
        m = jnp.minimum(s, steps - 1)
        return m // nt, m % nt, 0

    def ffn_tile(s):
        f = jnp.maximum(s - 1, 0)
        return f // nt, f % nt, 0

    return mixer_tile, ffn_tile


def _even_layer(x, norm_w, w_in, w_out, lb, gnorm, ln_g, ln_b, sgu_w, sgu_b_bc,
                ffn_norm_w, w_gate, w_up, w_down):
    bsz, seq, _ = x.shape
    tt = min(MIX_TILE, seq)
    nt = seq // tt
    steps = bsz * nt
    mixer_tile, ffn_tile = _tile_index_maps(nt, steps)
    return pl.pallas_call(
        functools.partial(_even_kernel, tt=tt, nt=nt, steps=steps),
        grid=(steps + 1,),
        in_specs=[
            pl.BlockSpec((1, tt, D_MODEL), mixer_tile),
            _const_spec((1, D_MODEL)),
            _const_spec((D_MODEL, IN_EVEN)),
            _const_spec((D_MODEL, D_MODEL)),
            _const_spec((1, HGRN_K)),
            _const_spec((1, HGRN_V)),
            _const_spec((1, MIX_HALF)),
            _const_spec((1, MIX_HALF)),
            _const_spec((SGU_GROUPS, SGU_CHUNK, SGU_CHUNK)),
            _const_spec((SGU_GROUPS, SGU_CHUNK, SGU_CH)),
            _const_spec((1, D_MODEL)),
            _const_spec((D_MODEL, D_FF)),
            _const_spec((D_MODEL, D_FF)),
            _const_spec((D_FF, D_MODEL)),
        ],
        out_specs=pl.BlockSpec((1, tt, D_MODEL), ffn_tile),
        out_shape=jax.ShapeDtypeStruct(x.shape, F32),
        scratch_shapes=[
            pltpu.VMEM((tt, IN_EVEN), F32),
            pltpu.VMEM((HGRN_HEADS, HGRN_DV, HGRN_DK), F32),
            pltpu.VMEM((HGRN_HEADS, SUBLANES + HGRN_CHUNK, HGRN_DK), F32),
            pltpu.VMEM((HGRN_HEADS, SUBLANES + HGRN_CHUNK, HGRN_DK), F32),
            pltpu.VMEM((len(HGRN_LEVELS) + HGRN_DIAG, HGRN_CHUNK, HGRN_CHUNK), F32),
            pltpu.VMEM((tt, D_MODEL), BF16),
            pltpu.VMEM((2, tt, D_MODEL), F32),
            pltpu.VMEM((tt, D_FF), BF16),
        ],
        compiler_params=pltpu.CompilerParams(
            dimension_semantics=("arbitrary",), vmem_limit_bytes=VMEM_LIMIT_BYTES),
        name="even_layer",
    )(x, norm_w, w_in, w_out, lb, gnorm, ln_g, ln_b, sgu_w, sgu_b_bc,
      ffn_norm_w, w_gate, w_up, w_down)


ODD_CQ = 2 * CONV_CH
ODD_CKV = ODD_CQ + MLA_Q_RANK
ODD_ROPE = ODD_CKV + MLA_KV_RANK
IN_ODD_EXT = ODD_ROPE + 2 * MLA_ROPE
Q_SLOT = MLA_NOPE + 2 * MLA_ROPE


def _odd_kernel(x_ref, pos_ref, nw_ref, win_ref, wout_ref, cw_ref, cb_ref, clg_ref, clb_ref,
                qn_ref, wuq_ref, kvn_ref, wukv_ref, inv_ref, phase_ref, sgn_ref, o_ref,
                p_scr, hpad, cacc_scr, k_scr, vt_scr, q_scr, m_scr, l_scr, acc_scr, s_scr, mix_scr,
                *, tt):
    t_idx = pl.program_id(1)
    scale = (MLA_NOPE + MLA_ROPE) ** -0.5

    @pl.when(t_idx == 0)
    def _():
        hpad[:, 0:CONV_HALO, :] = jnp.zeros((CONV_CH // LANES, CONV_HALO, LANES), F32)

    x = x_ref[0]
    h = _rms(x, nw_ref[...]).astype(BF16)
    p_scr[...] = _dot(h, win_ref[...])

    for cb in range(CONV_CH // LANES):
        cs = slice(cb * LANES, (cb + 1) * LANES)
        a = p_scr[:, cs]
        gate = p_scr[:, CONV_CH + cb * LANES:CONV_CH + (cb + 1) * LANES]
        hpad[cb, CONV_HALO:CONV_HALO + tt, :] = a * (0.5 + 0.5 * jnp.tanh(0.5 * gate))
    for r in range(0, tt, CONV_ROWS):
        for cb in range(CONV_CH // LANES):
            cs = slice(cb * LANES, (cb + 1) * LANES)
            acc = jnp.broadcast_to(cb_ref[:, cs], (CONV_ROWS, LANES))
            for w in range(CONV_WIDTH):
                r0w = r + CONV_OFF + w
                acc = acc + hpad[cb, r0w:r0w + CONV_ROWS, :] * cw_ref[w:w + 1, cs]
            cacc_scr[r:r + CONV_ROWS, cs] = acc
    for cb in range(CONV_CH // LANES):
        hpad[cb, 0:CONV_HALO, :] = hpad[cb, tt:tt + CONV_HALO, :]
    acc = cacc_scr[...]
    mu = jnp.mean(acc, axis=-1, keepdims=True)
    ac = acc - mu
    var = jnp.mean(ac * ac, axis=-1, keepdims=True)
    cn = ac * lax.rsqrt(var + EPS) * clg_ref[...] + clb_ref[...]
    mix_scr[:, 0:CONV_CH] = _silu(cn).astype(BF16)

    cq = _rms(p_scr[:, ODD_CQ:ODD_CKV], qn_ref[...]).astype(BF16)
    ckv = _rms(p_scr[:, ODD_CKV:ODD_ROPE], kvn_ref[...]).astype(BF16)
    qf = _dot(cq, wuq_ref[...])
    kvf = _dot(ckv, wukv_ref[...])

    ang = pos_ref[0] * inv_ref[...] + phase_ref[...]
    rot = jnp.sin(ang) * sgn_ref[...]
    kr = p_scr[:, ODD_ROPE:IN_ODD_EXT] * rot
    kr = kr + pltpu.roll(kr, MLA_ROPE, axis=1)
    q_mult = jnp.concatenate([jnp.full((tt, MLA_NOPE), scale, F32), rot * scale], axis=1)

    r0 = pl.multiple_of(t_idx * tt, tt)
    for hd in range(MLA_HEADS):
        q_scr[hd] = (qf[:, hd * Q_SLOT:(hd + 1) * Q_SLOT] * q_mult).astype(BF16)
        kv0 = hd * (MLA_NOPE + MLA_V)
        k_scr[hd, pl.ds(r0, tt), 0:MLA_NOPE] = kvf[:, kv0:kv0 + MLA_NOPE].astype(BF16)
        k_scr[hd, pl.ds(r0, tt), MLA_NOPE:Q_SLOT] = kr.astype(BF16)
        vt_scr[hd, t_idx] = kvf[:, kv0 + MLA_NOPE:kv0 + MLA_NOPE + MLA_V].T.astype(BF16)
        m_scr[hd] = jnp.full((1, tt), NEG_BIG, F32)
        l_scr[hd] = jnp.zeros((1, tt), F32)
        acc_scr[hd] = jnp.zeros((MLA_V, tt), F32)

    heads = range(MLA_HEADS)

    def scores_into(kb, slot):
        k0 = pl.multiple_of(kb * tt, tt)
        for hd in heads:
            s_scr[slot, hd] = _dot_nt(k_scr[hd, pl.ds(k0, tt), :], q_scr[hd])

    def attend(kb, masked):
        sts = [s_scr[kb % 2, hd] for hd in heads]
        if not masked:
            scores_into(kb + 1, (kb + 1) % 2)
        if masked:
            krow = lax.broadcasted_iota(jnp.int32, (tt, tt), 0)
            qcol = lax.broadcasted_iota(jnp.int32, (tt, tt), 1)
            sts = [jnp.where(krow <= qcol, st, NEG_BIG) for st in sts]
        m_olds = [m_scr[hd] for hd in heads]
        m_news = [jnp.maximum(m_olds[hd], jnp.max(sts[hd], axis=0, keepdims=True)) for hd in heads]
        alphas = [jnp.exp(m_olds[hd] - m_news[hd]) for hd in heads]
        prs = [jnp.exp(sts[hd] - m_news[hd]) for hd in heads]
        for hd in heads:
            l_scr[hd] = alphas[hd] * l_scr[hd] + jnp.sum(prs[hd], axis=0, keepdims=True)
            m_scr[hd] = m_news[hd]
        pvs = [_dot(vt_scr[hd, kb], prs[hd].astype(BF16)) for hd in heads]
        for hd in heads:
            acc_scr[hd] = acc_scr[hd] * alphas[hd] + pvs[hd]

    def kv_body(kb, carry):
        attend(kb, False)
        return carry

    scores_into(0, 0)
    lax.fori_loop(0, t_idx, kv_body, 0)
    attend(t_idx, True)
    for hd in range(MLA_HEADS):
        o_t = acc_scr[hd] * (1.0 / l_scr[hd])
        mix_scr[:, CONV_CH + hd * MLA_V:CONV_CH + (hd + 1) * MLA_V] = o_t.T.astype(BF16)

    o_ref[0] = x + _dot(mix_scr[...], wout_ref[...])


def _odd_mixer(x, pos_bc, norm_w, w_in_ext, w_out, conv_w, conv_b, cln_g, cln_b,
               q_norm, w_uq_ext, kv_norm, w_ukv, inv_row, phase_row, sgn_row):
    bsz, seq, _ = x.shape
    tt = min(ODD_TILE, seq)
    return pl.pallas_call(
        functools.partial(_odd_kernel, tt=tt),
        grid=(bsz, seq // tt),
        in_specs=[
            pl.BlockSpec((1, tt, D_MODEL), lambda b, t: (b, t, 0)),
            pl.BlockSpec((1, tt, 2 * MLA_ROPE), lambda b, t: (b, t, 0)),
            _const_spec((1, D_MODEL)),
            _const_spec((D_MODEL, IN_ODD_EXT)),
            _const_spec((D_MODEL, D_MODEL)),
            _const_spec((CONV_WIDTH, CONV_CH)),
            _const_spec((1, CONV_CH)),
            _const_spec((1, CONV_CH)),
            _const_spec((1, CONV_CH)),
            _const_spec((1, MLA_Q_RANK)),
            _const_spec((MLA_Q_RANK, MLA_HEADS * Q_SLOT)),
            _const_spec((1, MLA_KV_RANK)),
            _const_spec((MLA_KV_RANK, MLA_HEADS * (MLA_NOPE + MLA_V))),
            _const_spec((1, 2 * MLA_ROPE)),
            _const_spec((1, 2 * MLA_ROPE)),
            _const_spec((1, 2 * MLA_ROPE)),
        ],
        out_specs=pl.BlockSpec((1, tt, D_MODEL), lambda b, t: (b, t, 0)),
        out_shape=jax.ShapeDtypeStruct(x.shape, F32),
        scratch_shapes=[
            pltpu.VMEM((tt, IN_ODD_EXT), F32),
            pltpu.VMEM((CONV_CH // LANES, CONV_HALO + tt, LANES), F32),
            pltpu.VMEM((tt, CONV_CH), F32),
            pltpu.VMEM((MLA_HEADS, seq, Q_SLOT), BF16),
            pltpu.VMEM((MLA_HEADS, seq // tt, MLA_V, tt), BF16),
            pltpu.VMEM((MLA_HEADS, tt, Q_SLOT), BF16),
            pltpu.VMEM((MLA_HEADS, 1, tt), F32),
            pltpu.VMEM((MLA_HEADS, 1, tt), F32),
            pltpu.VMEM((MLA_HEADS, MLA_V, tt), F32),
            pltpu.VMEM((2, MLA_HEADS, tt, tt), F32),
            pltpu.VMEM((tt, D_MODEL), BF16),
        ],
        compiler_params=pltpu.CompilerParams(
            dimension_semantics=("arbitrary", "arbitrary"), vmem_limit_bytes=VMEM_LIMIT_BYTES),
        name="odd_mixer",
    )(x, pos_bc, norm_w, w_in_ext, w_out, conv_w, conv_b, cln_g, cln_b,
      q_norm, w_uq_ext, kv_norm, w_ukv, inv_row, phase_row, sgn_row)


def _swap_halves(w):
    half = w.shape[-1] // 2
    return jnp.concatenate([w[..., half:], w[..., :half]], axis=-1)


def kernel(x, positions, mix_norm, ffn_norm, ffn_gate, ffn_up, ffn_down, w_in_even, w_out_even,
           hgrn_lb_logits, hgrn_gnorm, sgu_ln_g, sgu_ln_b, sgu_w, sgu_b, w_in_odd, w_out_odd,
           conv_w, conv_b, conv_ln_g, conv_ln_b, mla_q_norm, mla_w_uq, mla_kv_norm, mla_w_ukv,
           final_norm):
    bsz, seq, _ = x.shape
    depth = mix_norm.shape[0]
    lower_bounds = jnp.cumsum(jax.nn.softmax(hgrn_lb_logits.astype(F32), axis=0), axis=0)

    inv = 1.0 / (ROPE_THETA ** (jnp.arange(0, MLA_ROPE, 2, dtype=F32) / MLA_ROPE))
    inv_row = jnp.tile(inv, 4)[None, :]
    sgn_row = jnp.concatenate([jnp.ones((MLA_ROPE,), F32), -jnp.ones((MLA_ROPE // 2,), F32),
                               jnp.ones((MLA_ROPE // 2,), F32)])[None, :]
    phase_row = jnp.concatenate([jnp.full((MLA_ROPE,), math.pi / 2, F32),
                                 jnp.zeros((MLA_ROPE,), F32)])[None, :]
    pos_bc = jnp.broadcast_to(positions.astype(F32)[:, :, None], (bsz, seq, 2 * MLA_ROPE))

    row = lambda v: v.reshape(1, -1).astype(F32)
    for layer in range(depth):
        j = layer // 2
        if layer % 2 == 0:
            sgu_b_bc = jnp.broadcast_to(sgu_b[j][:, :, None], (SGU_GROUPS, SGU_CHUNK, SGU_CH))
            x = _even_layer(x, row(mix_norm[layer]), w_in_even[j].astype(BF16),
                            w_out_even[j].astype(BF16), row(lower_bounds[j]), row(hgrn_gnorm[j]),
                            row(sgu_ln_g[j]), row(sgu_ln_b[j]), sgu_w[j], sgu_b_bc,
                            row(ffn_norm[layer]), ffn_gate[layer].astype(BF16),
                            ffn_up[layer].astype(BF16), ffn_down[layer].astype(BF16))
            continue
        else:
            w_in = w_in_odd[j]
            w_in_ext = jnp.concatenate([w_in, _swap_halves(w_in[:, ODD_ROPE:])], axis=1).astype(BF16)
            wq = mla_w_uq[j].reshape(MLA_Q_RANK, MLA_HEADS, MLA_NOPE + MLA_ROPE)
            wq_ext = jnp.concatenate([wq, _swap_halves(wq[:, :, MLA_NOPE:])], axis=2)
            wq_ext = wq_ext.reshape(MLA_Q_RANK, MLA_HEADS * Q_SLOT).astype(BF16)
            x = _odd_mixer(x, pos_bc, row(mix_norm[layer]), w_in_ext, w_out_odd[j].astype(BF16),
                           conv_w[j], row(conv_b[j]), row(conv_ln_g[j]), row(conv_ln_b[j]),
                           row(mla_q_norm[j]), wq_ext, row(mla_kv_norm[j]),
                           mla_w_ukv[j].astype(BF16), inv_row, phase_row, sgn_row)
        x2 = _ffn(x.reshape(bsz * seq, D_MODEL), row(ffn_norm[layer]), ffn_gate[layer].astype(BF16),
                  ffn_up[layer].astype(BF16), ffn_down[layer].astype(BF16), row(final_norm),
                  final=(layer == depth - 1))
        x = x2.reshape(bsz, seq, D_MODEL)
    return x
```

```python
import functools
import math

import jax
import jax.numpy as jnp
from jax import lax
from jax.experimental import pallas as pl
from jax.experimental.pallas import tpu as pltpu

F32 = jnp.float32
BF16 = jnp.bfloat16

D_MODEL = 1024
MIX_HALF = D_MODEL // 2
HGRN_HEADS = 4
HGRN_DK = 128
HGRN_DV = MIX_HALF // HGRN_HEADS
HGRN_K = HGRN_HEADS * HGRN_DK
HGRN_V = HGRN_HEADS * HGRN_DV
SGU_GROUPS = 4
SGU_CH = MIX_HALF // SGU_GROUPS
SGU_CHUNK = 128
CONV_CH = MIX_HALF
CONV_WIDTH = 31
MLA_HEADS = 4
MLA_NOPE = 128
MLA_ROPE = 64
MLA_V = 128
MLA_Q_RANK = 384
MLA_KV_RANK = 256
ROPE_THETA = 10000.0
D_FF = -(-8 * D_MODEL // (3 * 256)) * 256
EPS = 1e-6
IN_EVEN = 2 * HGRN_K + 2 * HGRN_V + 2 * MIX_HALF
IN_ODD = 2 * CONV_CH + MLA_Q_RANK + MLA_KV_RANK + MLA_ROPE

LANES = 128
SUBLANES = 8
VMEM_LIMIT_BYTES = 56 * 1024 * 1024

MIX_TILE = 256
ODD_TILE = 512
FFN_TILE = 512
FFN_CHUNKS = ((0, 768), (768, 1536), (1536, 2304), (2304, D_FF))
FFN_UP_BLOCK = 512
FFN_DOWN_BLOCK = 512
HGRN_CHUNK = 64
HGRN_LEVELS = (32, 16, 8, 4)
HGRN_DIAG = 4
CONV_HALO = 32
CONV_OFF = CONV_HALO - (CONV_WIDTH - 1)
CONV_ROWS = 32
NEG_BIG = -1e30


def _rms(x, w):
    return x * lax.rsqrt(jnp.mean(x * x, axis=-1, keepdims=True) + EPS) * w


def _dot(a, b):
    return jnp.dot(a, b, preferred_element_type=F32)


def _dot_nt(a, b):
    return lax.dot_general(a, b, (((1,), (1,)), ((), ())), preferred_element_type=F32)


def _sigmoid(x):
    return 1.0 / (1.0 + jnp.exp(-x))


def _silu(x):
    hx = 0.5 * x
    return hx + hx * jnp.tanh(hx)


def _gelu_tanh(x):
    return 0.5 * x * (1.0 + jnp.tanh(math.sqrt(2.0 / math.pi) * (x + 0.044715 * (x * x * x))))


def _const_spec(shape):
    nd = len(shape)
    return pl.BlockSpec(shape, lambda *_: (0,) * nd, pipeline_mode=pl.Buffered(1))


class _TrailingFfn:
    def __init__(self, x_prev, nw_ref, wg_ref, wu_ref, wd_ref, act_scr, out_fn):
        self.x, self.wg_ref, self.wu_ref, self.wd_ref = x_prev, wg_ref, wu_ref, wd_ref
        self.act_scr, self.out_fn = act_scr, out_fn
        self.h = _rms(x_prev, nw_ref[...]).astype(BF16)
        up = [(self._up, c0) for c0 in range(0, D_FF, FFN_UP_BLOCK)]
        down = [(self._down, n0) for n0 in range(0, D_MODEL, FFN_DOWN_BLOCK)]
        self.todo = up + down

    def _up(self, c0):
        cols = slice(c0, min(c0 + FFN_UP_BLOCK, D_FF))
        g = _dot(self.h, self.wg_ref[:, cols])
        u = _dot(self.h, self.wu_ref[:, cols])
        self.act_scr[:, cols] = (_silu(g) * u).astype(BF16)

    def _down(self, n0):
        cols = slice(n0, n0 + FFN_DOWN_BLOCK)
        self.out_fn(cols, self.x[:, cols] + _dot(self.act_scr[...], self.wd_ref[:, cols]))

    def step(self, n=1):
        for _ in range(min(n, len(self.todo))):
            fn, arg = self.todo.pop(0)
            fn(arg)

    def finish(self):
        self.step(len(self.todo))


def _ffn_kernel(x_ref, nw_ref, wg_ref, wu_ref, wd_ref, fw_ref, o_ref, *, final):
    x = x_ref[...]
    h = _rms(x, nw_ref[...]).astype(BF16)
    acc = x
    for c0, c1 in FFN_CHUNKS:
        g = _dot(h, wg_ref[:, c0:c1])
        u = _dot(h, wu_ref[:, c0:c1])
        a = (_silu(g) * u).astype(BF16)
        acc = acc + _dot(a, wd_ref[c0:c1, :])
    if final:
        acc = _rms(acc, fw_ref[...])
    o_ref[...] = acc


def _ffn(x2, norm_w, w_gate, w_up, w_down, final_w, *, final):
    n = x2.shape[0]
    tm = min(FFN_TILE, n)
    return pl.pallas_call(
        functools.partial(_ffn_kernel, final=final),
        grid=(n // tm,),
        in_specs=[
            pl.BlockSpec((tm, D_MODEL), lambda i: (i, 0)),
            _const_spec((1, D_MODEL)),
            _const_spec((D_MODEL, D_FF)),
            _const_spec((D_MODEL, D_FF)),
            _const_spec((D_FF, D_MODEL)),
            _const_spec((1, D_MODEL)),
        ],
        out_specs=pl.BlockSpec((tm, D_MODEL), lambda i: (i, 0)),
        out_shape=jax.ShapeDtypeStruct((n, D_MODEL), F32),
        compiler_params=pltpu.CompilerParams(
            dimension_semantics=("arbitrary",), vmem_limit_bytes=VMEM_LIMIT_BYTES),
        name="ffn_final" if final else "ffn",
    )(x2, norm_w, w_gate, w_up, w_down, final_w)


def _even_kernel(x_ref, nw_ref, win_ref, wout_ref, lb_ref, gn_ref, lng_ref, lnb_ref,
                 sw_ref, sb_ref, fnw_ref, wg_ref, wu_ref, wd_ref, o_ref,
                 p_scr, st_scr, kpad, bpad, msk_scr, mix_scr, xm_scr, act_scr, *, tt, nt, steps):
    s_idx = pl.program_id(0)
    t_idx = lax.rem(jnp.minimum(s_idx, steps - 1), nt)
    slot = lax.rem(s_idx, 2)
    C, PAD = HGRN_CHUNK, SUBLANES
    heads = range(HGRN_HEADS)

    @pl.when(s_idx == 0)
    def _():
        xm_scr[1] = jnp.zeros((tt, D_MODEL), F32)

    @pl.when(t_idx == 0)
    def _():
        st_scr[...] = jnp.zeros_like(st_scr)
        kpad[:, 0:PAD, :] = jnp.zeros((HGRN_HEADS, PAD, HGRN_DK), F32)
        bpad[:, 0:PAD, :] = jnp.zeros((HGRN_HEADS, PAD, HGRN_DK), F32)

    def ffn_out(cols, val):
        o_ref[0, :, cols] = val

    ffn = _TrailingFfn(xm_scr[1 - slot], fnw_ref, wg_ref, wu_ref, wd_ref, act_scr, ffn_out)
    x = x_ref[0]
    h = _rms(x, nw_ref[...]).astype(BF16)
    p_scr[...] = _dot(h, win_ref[...])

    lb = lb_ref[...]
    ti = lax.broadcasted_iota(jnp.int32, (C, C), 0)
    si = lax.broadcasted_iota(jnp.int32, (C, C), 1)
    tril_c = jnp.where(ti >= si, 1.0, 0.0).astype(BF16)
    tril3 = jnp.concatenate([tril_c, tril_c, tril_c], axis=1)
    xr = jnp.bitwise_xor(ti, si)
    for li, hs in enumerate(HGRN_LEVELS):
        own = jnp.logical_and(lax.shift_right_logical(xr, hs.bit_length() - 1) == 1, ti > si)
        msk_scr[li] = jnp.where(own, 1.0, 0.0)
    for d in range(HGRN_DIAG):
        own = jnp.logical_and(ti - si == d, xr < HGRN_DIAG)
        msk_scr[len(HGRN_LEVELS) + d] = jnp.where(own, 1.0, 0.0)

    for c in range(tt // C):
        ffn.step()
        rows = slice(c * C, (c + 1) * C)
        q_all = p_scr[rows, 0:HGRN_K]
        f_pre = p_scr[rows, HGRN_K:2 * HGRN_K]
        iv_all = p_scr[rows, 2 * HGRN_K:2 * HGRN_K + HGRN_V]
        g_all = p_scr[rows, 2 * HGRN_K + HGRN_V:2 * HGRN_K + 2 * HGRN_V]
        f = lb + (1.0 - lb) * _sigmoid(f_pre)
        lf = jnp.log(f)
        lf_hi = lf.astype(BF16)
        lf_r = lf - lf_hi.astype(F32)
        lf_mid = lf_r.astype(BF16)
        lf_lo = (lf_r - lf_mid.astype(F32)).astype(BF16)
        b_all = _dot(tril3, jnp.concatenate([lf_hi, lf_mid, lf_lo], axis=0))
        k_all = 1.0 - f
        for hd in heads:
            kpad[hd, PAD:PAD + C, :] = k_all[:, hd * HGRN_DK:(hd + 1) * HGRN_DK]
            bpad[hd, PAD:PAD + C, :] = b_all[:, hd * HGRN_DK:(hd + 1) * HGRN_DK]

        def rows_of(ref, start, size):
            return jnp.concatenate([ref[hd, start:start + size, :] for hd in heads], axis=1)

        b_last = rows_of(bpad, PAD + C - 1, 1)
        q_inter = (q_all * jnp.exp(b_all)).astype(BF16)
        k_dec = (k_all * jnp.exp(b_last - b_all)).astype(BF16)
        dec_row = jnp.exp(b_last)
        iv_bf = iv_all.astype(BF16)
        gate_all = _silu(g_all)

        lev = []
        for hs in HGRN_LEVELS:
            pivots = [blk * 2 * hs + hs - 1 for blk in range(C // (2 * hs))]
            piv = jnp.concatenate(
                [jnp.broadcast_to(rows_of(bpad, PAD + p, 1), (2 * hs, HGRN_K)) for p in pivots],
                axis=0)
            e = jnp.exp(-jnp.abs(b_all - piv))
            lev.append(((q_all * e).astype(BF16), (k_all * e).astype(BF16)))
        diag = [q_all * k_all]
        for d in range(1, HGRN_DIAG):
            k_sh = rows_of(kpad, PAD - d, C)
            b_sh = rows_of(bpad, PAD - d, C)
            diag.append(q_all * k_sh * jnp.exp(b_all - b_sh))

        for hd in heads:
            if hd % 2 == 0:
                ffn.step()
            sl = slice(hd * HGRN_DK, (hd + 1) * HGRN_DK)
            vsl = slice(hd * HGRN_DV, (hd + 1) * HGRN_DV)
            sc = jnp.zeros((C, C), F32)
            for li in range(len(HGRN_LEVELS)):
                sc = sc + _dot_nt(lev[li][0][:, sl], lev[li][1][:, sl]) * msk_scr[li]
            for d in range(HGRN_DIAG):
                sc = sc + (jnp.sum(diag[d][:, sl], axis=-1, keepdims=True)
                           * msk_scr[len(HGRN_LEVELS) + d])
            st = st_scr[hd]
            o = _dot_nt(q_inter[:, sl], st.astype(BF16)) + _dot(sc.astype(BF16), iv_bf[:, vsl])
            st_scr[hd] = st * dec_row[:, sl] + _dot(iv_all[:, vsl].T.astype(BF16), k_dec[:, sl])
            on = _rms(o, gn_ref[:, vsl])
            mix_scr[rows, vsl] = (on * gate_all[:, vsl]).astype(BF16)

    prow = lax.broadcasted_iota(jnp.int32, (SGU_CHUNK, SGU_CHUNK), 0)
    pcol = lax.broadcasted_iota(jnp.int32, (SGU_CHUNK, SGU_CHUNK), 1)
    w_causal = [jnp.where(prow >= pcol, sw_ref[gi], 0.0).astype(BF16) for gi in range(SGU_GROUPS)]
    for n in range(tt // SGU_CHUNK):
        rows = slice(n * SGU_CHUNK, (n + 1) * SGU_CHUNK)
        for gi in range(SGU_GROUPS):
            if gi % 2 == 0:
                ffn.step()
            csl = slice(gi * SGU_CH, (gi + 1) * SGU_CH)
            u = _gelu_tanh(p_scr[rows, 2 * HGRN_K + 2 * HGRN_V + gi * SGU_CH:
                                 2 * HGRN_K + 2 * HGRN_V + (gi + 1) * SGU_CH])
            v = _gelu_tanh(p_scr[rows, 2 * HGRN_K + 2 * HGRN_V + MIX_HALF + gi * SGU_CH:
                                 2 * HGRN_K + 2 * HGRN_V + MIX_HALF + (gi + 1) * SGU_CH])
            mu = jnp.mean(v, axis=-1, keepdims=True)
            vc = v - mu
            var = jnp.mean(vc * vc, axis=-1, keepdims=True)
            vn = vc * lax.rsqrt(var + EPS) * lng_ref[:, csl] + lnb_ref[:, csl]
            z = _dot(w_causal[gi], vn.astype(BF16)) + sb_ref[gi]
            mix_scr[rows, HGRN_V + gi * SGU_CH:HGRN_V + (gi + 1) * SGU_CH] = (u * z).astype(BF16)

    xm_scr[slot] = x + _dot(mix_scr[...], wout_ref[...])
    ffn.finish()


def _tile_index_maps(nt, steps):
    def mixer_tile(s):
        m = jnp.minimum(s, steps - 1)
        return m // nt, m % nt, 0

    def ffn_tile(s):
        f = jnp.maximum(s - 1, 0)
        return f // nt, f % nt, 0

    return mixer_tile, ffn_tile


def _even_layer(x, norm_w, w_in, w_out, lb, gnorm, ln_g, ln_b, sgu_w, sgu_b_bc,
                ffn_norm_w, w_gate, w_up, w_down):
    bsz, seq, _ = x.shape
    tt = min(MIX_TILE, seq)
    nt = seq // tt
    steps = bsz * nt
    mixer_tile, ffn_tile = _tile_index_maps(nt, steps)
    return pl.pallas_call(
        functools.partial(_even_kernel, tt=tt, nt=nt, steps=steps),
        grid=(steps + 1,),
        in_specs=[
            pl.BlockSpec((1, tt, D_MODEL), mixer_tile),
            _const_spec((1, D_MODEL)),
            _const_spec((D_MODEL, IN_EVEN)),
            _const_spec((D_MODEL, D_MODEL)),
            _const_spec((1, HGRN_K)),
            _const_spec((1, HGRN_V)),
            _const_spec((1, MIX_HALF)),
            _const_spec((1, MIX_HALF)),
            _const_spec((SGU_GROUPS, SGU_CHUNK, SGU_CHUNK)),
            _const_spec((SGU_GROUPS, SGU_CHUNK, SGU_CH)),
            _const_spec((1, D_MODEL)),
            _const_spec((D_MODEL, D_FF)),
            _const_spec((D_MODEL, D_FF)),
            _const_spec((D_FF, D_MODEL)),
        ],
        out_specs=pl.BlockSpec((1, tt, D_MODEL), ffn_tile),
        out_shape=jax.ShapeDtypeStruct(x.shape, F32),
        scratch_shapes=[
            pltpu.VMEM((tt, IN_EVEN), F32),
            pltpu.VMEM((HGRN_HEADS, HGRN_DV, HGRN_DK), F32),
            pltpu.VMEM((HGRN_HEADS, SUBLANES + HGRN_CHUNK, HGRN_DK), F32),
            pltpu.VMEM((HGRN_HEADS, SUBLANES + HGRN_CHUNK, HGRN_DK), F32),
            pltpu.VMEM((len(HGRN_LEVELS) + HGRN_DIAG, HGRN_CHUNK, HGRN_CHUNK), F32),
            pltpu.VMEM((tt, D_MODEL), BF16),
            pltpu.VMEM((2, tt, D_MODEL), F32),
            pltpu.VMEM((tt, D_FF), BF16),
        ],
        compiler_params=pltpu.CompilerParams(
            dimension_semantics=("arbitrary",), vmem_limit_bytes=VMEM_LIMIT_BYTES),
        name="even_layer",
    )(x, norm_w, w_in, w_out, lb, gnorm, ln_g, ln_b, sgu_w, sgu_b_bc,
      ffn_norm_w, w_gate, w_up, w_down)


ODD_CQ = 2 * CONV_CH
ODD_CKV = ODD_CQ + MLA_Q_RANK
ODD_ROPE = ODD_CKV + MLA_KV_RANK
IN_ODD_EXT = ODD_ROPE + 2 * MLA_ROPE
Q_SLOT = MLA_NOPE + 2 * MLA_ROPE


def _odd_kernel(x_ref, pos_ref, nw_ref, win_ref, wout_ref, cw_ref, cb_ref, clg_ref, clb_ref,
                qn_ref, wuq_ref, kvn_ref, wukv_ref, inv_ref, phase_ref, sgn_ref, o_ref,
                p_scr, hpad, cacc_scr, k_scr, vt_scr, q_scr, m_scr, l_scr, acc_scr, mix_scr,
                *, tt):
    t_idx = pl.program_id(1)
    scale = (MLA_NOPE + MLA_ROPE) ** -0.5

    @pl.when(t_idx == 0)
    def _():
        hpad[:, 0:CONV_HALO, :] = jnp.zeros((CONV_CH // LANES, CONV_HALO, LANES), F32)

    x = x_ref[0]
    h = _rms(x, nw_ref[...]).astype(BF16)
    p_scr[...] = _dot(h, win_ref[...])

    for cb in range(CONV_CH // LANES):
        cs = slice(cb * LANES, (cb + 1) * LANES)
        a = p_scr[:, cs]
        gate = p_scr[:, CONV_CH + cb * LANES:CONV_CH + (cb + 1) * LANES]
        hpad[cb, CONV_HALO:CONV_HALO + tt, :] = a * _sigmoid(gate)
    for r in range(0, tt, CONV_ROWS):
        for cb in range(CONV_CH // LANES):
            cs = slice(cb * LANES, (cb + 1) * LANES)
            acc = jnp.broadcast_to(cb_ref[:, cs], (CONV_ROWS, LANES))
            for w in range(CONV_WIDTH):
                r0w = r + CONV_OFF + w
                acc = acc + hpad[cb, r0w:r0w + CONV_ROWS, :] * cw_ref[w:w + 1, cs]
            cacc_scr[r:r + CONV_ROWS, cs] = acc
    for cb in range(CONV_CH // LANES):
        hpad[cb, 0:CONV_HALO, :] = hpad[cb, tt:tt + CONV_HALO, :]
    acc = cacc_scr[...]
    mu = jnp.mean(acc, axis=-1, keepdims=True)
    ac = acc - mu
    var = jnp.mean(ac * ac, axis=-1, keepdims=True)
    cn = ac * lax.rsqrt(var + EPS) * clg_ref[...] + clb_ref[...]
    mix_scr[:, 0:CONV_CH] = _silu(cn).astype(BF16)

    cq = _rms(p_scr[:, ODD_CQ:ODD_CKV], qn_ref[...]).astype(BF16)
    ckv = _rms(p_scr[:, ODD_CKV:ODD_ROPE], kvn_ref[...]).astype(BF16)
    qf = _dot(cq, wuq_ref[...])
    kvf = _dot(ckv, wukv_ref[...])

    ang = pos_ref[0] * inv_ref[...] + phase_ref[...]
    rot = jnp.sin(ang) * sgn_ref[...]
    kr = p_scr[:, ODD_ROPE:IN_ODD_EXT] * rot
    kr = kr + pltpu.roll(kr, MLA_ROPE, axis=1)
    q_mult = jnp.concatenate([jnp.full((tt, MLA_NOPE), scale, F32), rot * scale], axis=1)

    r0 = pl.multiple_of(t_idx * tt, tt)
    for hd in range(MLA_HEADS):
        q_scr[hd] = (qf[:, hd * Q_SLOT:(hd + 1) * Q_SLOT] * q_mult).astype(BF16)
        kv0 = hd * (MLA_NOPE + MLA_V)
        k_scr[hd, pl.ds(r0, tt), 0:MLA_NOPE] = kvf[:, kv0:kv0 + MLA_NOPE].astype(BF16)
        k_scr[hd, pl.ds(r0, tt), MLA_NOPE:Q_SLOT] = kr.astype(BF16)
        vt_scr[hd, t_idx] = kvf[:, kv0 + MLA_NOPE:kv0 + MLA_NOPE + MLA_V].T.astype(BF16)
        m_scr[hd] = jnp.full((1, tt), NEG_BIG, F32)
        l_scr[hd] = jnp.zeros((1, tt), F32)
        acc_scr[hd] = jnp.zeros((MLA_V, tt), F32)

    heads = range(MLA_HEADS)

    def attend(kb, masked):
        k0 = pl.multiple_of(kb * tt, tt)
        sts = [_dot_nt(k_scr[hd, pl.ds(k0, tt), :], q_scr[hd]) for hd in heads]
        if masked:
            krow = lax.broadcasted_iota(jnp.int32, (tt, tt), 0)
            qcol = lax.broadcasted_iota(jnp.int32, (tt, tt), 1)
            sts = [jnp.where(krow <= qcol, st, NEG_BIG) for st in sts]
        m_olds = [m_scr[hd] for hd in heads]
        m_news = [jnp.maximum(m_olds[hd], jnp.max(sts[hd], axis=0, keepdims=True)) for hd in heads]
        alphas = [jnp.exp(m_olds[hd] - m_news[hd]) for hd in heads]
        prs = [jnp.exp(sts[hd] - m_news[hd]) for hd in heads]
        for hd in heads:
            l_scr[hd] = alphas[hd] * l_scr[hd] + jnp.sum(prs[hd], axis=0, keepdims=True)
            m_scr[hd] = m_news[hd]
        pvs = [_dot(vt_scr[hd, kb], prs[hd].astype(BF16)) for hd in heads]
        for hd in heads:
            acc_scr[hd] = acc_scr[hd] * alphas[hd] + pvs[hd]

    def kv_body(kb, carry):
        attend(kb, False)
        return carry

    lax.fori_loop(0, t_idx, kv_body, 0)
    attend(t_idx, True)
    for hd in range(MLA_HEADS):
        o_t = acc_scr[hd] * (1.0 / l_scr[hd])
        mix_scr[:, CONV_CH + hd * MLA_V:CONV_CH + (hd + 1) * MLA_V] = o_t.T.astype(BF16)

    o_ref[0] = x + _dot(mix_scr[...], wout_ref[...])


def _odd_mixer(x, pos_bc, norm_w, w_in_ext, w_out, conv_w, conv_b, cln_g, cln_b,
               q_norm, w_uq_ext, kv_norm, w_ukv, inv_row, phase_row, sgn_row):
    bsz, seq, _ = x.shape
    tt = min(ODD_TILE, seq)
    return pl.pallas_call(
        functools.partial(_odd_kernel, tt=tt),
        grid=(bsz, seq // tt),
        in_specs=[
            pl.BlockSpec((1, tt, D_MODEL), lambda b, t: (b, t, 0)),
            pl.BlockSpec((1, tt, 2 * MLA_ROPE), lambda b, t: (b, t, 0)),
            _const_spec((1, D_MODEL)),
            _const_spec((D_MODEL, IN_ODD_EXT)),
            _const_spec((D_MODEL, D_MODEL)),
            _const_spec((CONV_WIDTH, CONV_CH)),
            _const_spec((1, CONV_CH)),
            _const_spec((1, CONV_CH)),
            _const_spec((1, CONV_CH)),
            _const_spec((1, MLA_Q_RANK)),
            _const_spec((MLA_Q_RANK, MLA_HEADS * Q_SLOT)),
            _const_spec((1, MLA_KV_RANK)),
            _const_spec((MLA_KV_RANK, MLA_HEADS * (MLA_NOPE + MLA_V))),
            _const_spec((1, 2 * MLA_ROPE)),
            _const_spec((1, 2 * MLA_ROPE)),
            _const_spec((1, 2 * MLA_ROPE)),
        ],
        out_specs=pl.BlockSpec((1, tt, D_MODEL), lambda b, t: (b, t, 0)),
        out_shape=jax.ShapeDtypeStruct(x.shape, F32),
        scratch_shapes=[
            pltpu.VMEM((tt, IN_ODD_EXT), F32),
            pltpu.VMEM((CONV_CH // LANES, CONV_HALO + tt, LANES), F32),
            pltpu.VMEM((tt, CONV_CH), F32),
            pltpu.VMEM((MLA_HEADS, seq, Q_SLOT), BF16),
            pltpu.VMEM((MLA_HEADS, seq // tt, MLA_V, tt), BF16),
            pltpu.VMEM((MLA_HEADS, tt, Q_SLOT), BF16),
            pltpu.VMEM((MLA_HEADS, 1, tt), F32),
            pltpu.VMEM((MLA_HEADS, 1, tt), F32),
            pltpu.VMEM((MLA_HEADS, MLA_V, tt), F32),
            pltpu.VMEM((tt, D_MODEL), BF16),
        ],
        compiler_params=pltpu.CompilerParams(
            dimension_semantics=("arbitrary", "arbitrary"), vmem_limit_bytes=VMEM_LIMIT_BYTES),
        name="odd_mixer",
    )(x, pos_bc, norm_w, w_in_ext, w_out, conv_w, conv_b, cln_g, cln_b,
      q_norm, w_uq_ext, kv_norm, w_ukv, inv_row, phase_row, sgn_row)


def _swap_halves(w):
    half = w.shape[-1] // 2
    return jnp.concatenate([w[..., half:], w[..., :half]], axis=-1)


def kernel(x, positions, mix_norm, ffn_norm, ffn_gate, ffn_up, ffn_down, w_in_even, w_out_even,
           hgrn_lb_logits, hgrn_gnorm, sgu_ln_g, sgu_ln_b, sgu_w, sgu_b, w_in_odd, w_out_odd,
           conv_w, conv_b, conv_ln_g, conv_ln_b, mla_q_norm, mla_w_uq, mla_kv_norm, mla_w_ukv,
           final_norm):
    bsz, seq, _ = x.shape
    depth = mix_norm.shape[0]
    lower_bounds = jnp.cumsum(jax.nn.softmax(hgrn_lb_logits.astype(F32), axis=0), axis=0)

    inv = 1.0 / (ROPE_THETA ** (jnp.arange(0, MLA_ROPE, 2, dtype=F32) / MLA_ROPE))
    inv_row = jnp.tile(inv, 4)[None, :]
    sgn_row = jnp.concatenate([jnp.ones((MLA_ROPE,), F32), -jnp.ones((MLA_ROPE // 2,), F32),
                               jnp.ones((MLA_ROPE // 2,), F32)])[None, :]
    phase_row = jnp.concatenate([jnp.full((MLA_ROPE,), math.pi / 2, F32),
                                 jnp.zeros((MLA_ROPE,), F32)])[None, :]
    pos_bc = jnp.broadcast_to(positions.astype(F32)[:, :, None], (bsz, seq, 2 * MLA_ROPE))

    row = lambda v: v.reshape(1, -1).astype(F32)
    for layer in range(depth):
        j = layer // 2
        if layer % 2 == 0:
            sgu_b_bc = jnp.broadcast_to(sgu_b[j][:, :, None], (SGU_GROUPS, SGU_CHUNK, SGU_CH))
            x = _even_layer(x, row(mix_norm[layer]), w_in_even[j].astype(BF16),
                            w_out_even[j].astype(BF16), row(lower_bounds[j]), row(hgrn_gnorm[j]),
                            row(sgu_ln_g[j]), row(sgu_ln_b[j]), sgu_w[j], sgu_b_bc,
                            row(ffn_norm[layer]), ffn_gate[layer].astype(BF16),
                            ffn_up[layer].astype(BF16), ffn_down[layer].astype(BF16))
            continue
        else:
            w_in = w_in_odd[j]
            w_in_ext = jnp.concatenate([w_in, _swap_halves(w_in[:, ODD_ROPE:])], axis=1).astype(BF16)
            wq = mla_w_uq[j].reshape(MLA_Q_RANK, MLA_HEADS, MLA_NOPE + MLA_ROPE)
            wq_ext = jnp.concatenate([wq, _swap_halves(wq[:, :, MLA_NOPE:])], axis=2)
            wq_ext = wq_ext.reshape(MLA_Q_RANK, MLA_HEADS * Q_SLOT).astype(BF16)
            x = _odd_mixer(x, pos_bc, row(mix_norm[layer]), w_in_ext, w_out_odd[j].astype(BF16),
                           conv_w[j], row(conv_b[j]), row(conv_ln_g[j]), row(conv_ln_b[j]),
                           row(mla_q_norm[j]), wq_ext, row(mla_kv_norm[j]),
                           mla_w_ukv[j].astype(BF16), inv_row, phase_row, sgn_row)
        x2 = _ffn(x.reshape(bsz * seq, D_MODEL), row(ffn_norm[layer]), ffn_gate[layer].astype(BF16),
                  ffn_up[layer].astype(BF16), ffn_down[layer].astype(BF16), row(final_norm),
                  final=(layer == depth - 1))
        x = x2.reshape(bsz, seq, D_MODEL)
    return x
```
